```python
import math
import jax, jax.numpy as jnp
from jax import lax
import numpy as np

D_MODEL = 2048
BATCH = 1
SEQ = 16384
DEPTH = 2

CHUNK = 64
Q_BLOCK = 128
N_A = DEPTH // 2
N_B = DEPTH - N_A
CONV_E = D_MODEL
CONV_WIDTH = 31
DIFF_HEADS = 8
DIFF_HEAD_DIM = 128
DIFF_QK = DIFF_HEADS * 2 * DIFF_HEAD_DIM
DIFF_V = DIFF_HEADS * 2 * DIFF_HEAD_DIM
EPS = 1e-6

kernel_name = "yoco_conformer_conv_diff_attention"


def _rms_norm(x, g):
    xf = x.astype(jnp.float32)
    y = xf * lax.rsqrt(jnp.mean(xf * xf, axis=-1, keepdims=True) + EPS)
    return (y * g.astype(jnp.float32)).astype(x.dtype)


def _layer_norm(x, g, b):
    xf = x.astype(jnp.float32)
    mu = jnp.mean(xf, axis=-1, keepdims=True)
    xc = xf - mu
    y = xc * lax.rsqrt(jnp.mean(xc * xc, axis=-1, keepdims=True) + EPS)
    return (y * g.astype(jnp.float32) + b.astype(jnp.float32)).astype(x.dtype)


def _depthwise_causal_conv(y, w, b):
    e = y.shape[-1]
    out = lax.conv_general_dilated(
        y, w.reshape(CONV_WIDTH, 1, e).astype(y.dtype),
        window_strides=(1,), padding=[(CONV_WIDTH - 1, 0)],
        dimension_numbers=("NWC", "WIO", "NWC"), feature_group_count=e)
    return out + b.astype(y.dtype)


def _conformer_layer(x, norm_g, w_in, dw_w, dw_b, ln_g, ln_b, w_out):
    h = _rms_norm(x, norm_g)
    u = h @ w_in
    a, b, z = jnp.split(u, 3, axis=-1)
    y = a * jax.nn.sigmoid(b)
    y = _depthwise_causal_conv(y, dw_w, dw_b)
    y = jax.nn.silu(_layer_norm(y, ln_g, ln_b))
    y = y * jax.nn.silu(z)
    return x + y @ w_out


def _diff_attention(q, k, v, lam):
    bsz, s, h, _, dh = q.shape
    nb = s // Q_BLOCK
    qb = q.reshape(bsz, nb, Q_BLOCK, h, 2, dh).transpose(1, 0, 2, 3, 4, 5)
    key_chunk = jnp.arange(s) // CHUNK

    def block(args):
        q_blk, start = args
        q_chunk = (start + jnp.arange(Q_BLOCK)) // CHUNK
        mask = key_chunk[None, :] <= q_chunk[:, None]
        sc = jnp.einsum("bqhmd,bkhmd->bhmqk", q_blk, k,
                        preferred_element_type=jnp.float32)
        sc = jnp.where(mask, sc, -jnp.inf)
        p = jax.nn.softmax(sc, axis=-1).astype(v.dtype)
        o = jnp.einsum("bhmqk,bkhe->bqhme", p, v)
        return o[:, :, :, 0] - lam * o[:, :, :, 1]

    starts = jnp.arange(nb) * Q_BLOCK
    out = lax.map(block, (qb, starts))
    return out.transpose(1, 0, 2, 3, 4).reshape(bsz, s, h, 2 * dh)


def _diff_layer(x, k, v, norm_g, w_in, q_norm_g, lam_p, subln_g, w_out, layer_idx):
    bsz, s, _ = x.shape
    lam_init = 0.8 - 0.6 * math.exp(-0.3 * (layer_idx - 1))
    h = _rms_norm(x, norm_g)
    u = h @ w_in
    q = u[..., :DIFF_QK].reshape(bsz, s, DIFF_HEADS, 2, DIFF_HEAD_DIM)
    z = u[..., DIFF_QK:]
    q = _rms_norm(q, q_norm_g) * (DIFF_HEAD_DIM ** -0.5)
    lp = lam_p.astype(jnp.float32)
    lam = (jnp.exp(jnp.sum(lp[0] * lp[1])) - jnp.exp(jnp.sum(lp[2] * lp[3]))
           + lam_init).astype(x.dtype)
    o = _diff_attention(q, k, v, lam)
    o = _rms_norm(o, subln_g) * (1.0 - lam_init)
    o = o.reshape(bsz, s, DIFF_V) * jax.nn.silu(z)
    return x + o @ w_out


def setup_inputs(seed: int = 0) -> dict:
    key = jax.random.key(seed)
    ks = jax.random.split(key, 20)

    def w(k, shape, fan_in):
        return jax.random.normal(k, shape, jnp.float32) * (fan_in ** -0.5)

    def gain(k, shape):
        return 1.0 + 0.02 * jax.random.normal(k, shape, jnp.float32)

    def bias(k, shape):
        return 0.02 * jax.random.normal(k, shape, jnp.float32)

    return {
        "x": jax.random.normal(ks[0], (BATCH, SEQ, D_MODEL), jnp.float32),
        "a_norm_g": gain(ks[1], (N_A, D_MODEL)),
        "a_w_in": w(ks[2], (N_A, D_MODEL, 3 * CONV_E), D_MODEL),
        "a_dw_w": w(ks[3], (N_A, CONV_WIDTH, CONV_E), CONV_WIDTH),
        "a_dw_b": bias(ks[4], (N_A, CONV_E)),
        "a_ln_g": gain(ks[5], (N_A, CONV_E)),
        "a_ln_b": bias(ks[6], (N_A, CONV_E)),
        "a_w_out": w(ks[7], (N_A, CONV_E, D_MODEL), CONV_E),
        "kv_norm_g": gain(ks[8], (D_MODEL,)),
        "w_kv": w(ks[9], (D_MODEL, DIFF_QK + DIFF_V), D_MODEL),
        "k_norm_g": gain(ks[10], (DIFF_HEAD_DIM,)),
        "b_norm_g": gain(ks[11], (N_B, D_MODEL)),
        "b_w_in": w(ks[12], (N_B, D_MODEL, DIFF_QK + DIFF_V), D_MODEL),
        "b_q_norm_g": gain(ks[13], (N_B, DIFF_HEAD_DIM)),
        "b_lambda": 0.1 * jax.random.normal(ks[14], (N_B, 4, DIFF_HEAD_DIM), jnp.float32),
        "b_subln_g": gain(ks[15], (N_B, 2 * DIFF_HEAD_DIM)),
        "b_w_out": w(ks[16], (N_B, DIFF_V, D_MODEL), DIFF_V),
    }


def reference(x, a_norm_g, a_w_in, a_dw_w, a_dw_b, a_ln_g, a_ln_b, a_w_out,
              kv_norm_g, w_kv, k_norm_g, b_norm_g, b_w_in, b_q_norm_g,
              b_lambda, b_subln_g, b_w_out):
    bsz, s, _ = x.shape
    k = None
    v = None
    for layer in range(DEPTH):
        if layer < N_A:
            x = _conformer_layer(x, a_norm_g[layer], a_w_in[layer], a_dw_w[layer],
                                 a_dw_b[layer], a_ln_g[layer], a_ln_b[layer],
                                 a_w_out[layer])
        else:
            if layer == N_A:
                kv = _rms_norm(x, kv_norm_g) @ w_kv
                k = _rms_norm(kv[..., :DIFF_QK].reshape(
                    bsz, s, DIFF_HEADS, 2, DIFF_HEAD_DIM), k_norm_g)
                v = kv[..., DIFF_QK:].reshape(bsz, s, DIFF_HEADS, 2 * DIFF_HEAD_DIM)
            j = layer - N_A
            x = _diff_layer(x, k, v, b_norm_g[j], b_w_in[j], b_q_norm_g[j],
                            b_lambda[j], b_subln_g[j], b_w_out[j], layer + 1)
    return x
```

```python
import functools
import math

import jax
import jax.numpy as jnp
from jax import lax
from jax.experimental import pallas as pl
from jax.experimental.pallas import tpu as pltpu

EPS = 1e-6
CHUNK = 64
CONV_WIDTH = 31
HEAD_DIM = 128
V_DIM = 2 * HEAD_DIM
LANES = 128
HALO_ROWS = 32
VMEM_LIMIT = 56 * 1024 * 1024
LOG2E = 1.4426950408889634
NEG_BIG = -1e30

BF16 = jnp.bfloat16
F32 = jnp.float32


def _sigmoid(v):
    return 1.0 / (1.0 + jnp.exp(-v))


def _dot(a, b):
    return jnp.dot(a, b, preferred_element_type=F32)


def _params(*sem):
    return pltpu.CompilerParams(dimension_semantics=sem, vmem_limit_bytes=VMEM_LIMIT)


def _conf_in_kernel(x_ref, g_ref, wa_ref, wb_ref, wz_ref, y_ref, sz_ref, h_ref):
    @pl.when(pl.program_id(1) == 0)
    def _():
        x = x_ref[...]
        ms = jnp.mean(x * x, axis=-1, keepdims=True)
        h_ref[...] = (x * lax.rsqrt(ms + EPS) * g_ref[...]).astype(BF16)

    h = h_ref[...]
    a = _dot(h, wa_ref[...])
    b = _dot(h, wb_ref[...])
    z = _dot(h, wz_ref[...])
    y_ref[...] = (a * _sigmoid(b)).astype(BF16)
    sz_ref[...] = (z * _sigmoid(z)).astype(BF16)


def _conf_in(x, norm_g, w_in, *, tm=512, tn=512):
    s, d = x.shape
    e = w_in.shape[1] // 3
    nj = e // tn
    return pl.pallas_call(
        _conf_in_kernel,
        grid=(s // tm, nj),
        in_specs=[
            pl.BlockSpec((tm, d), lambda i, j: (i, 0)),
            pl.BlockSpec((1, d), lambda i, j: (0, 0)),
            pl.BlockSpec((d, tn), lambda i, j: (0, j)),
            pl.BlockSpec((d, tn), lambda i, j: (0, j + nj)),
            pl.BlockSpec((d, tn), lambda i, j: (0, j + 2 * nj)),
        ],
        out_specs=[
            pl.BlockSpec((tm, tn), lambda i, j: (i, j)),
            pl.BlockSpec((tm, tn), lambda i, j: (i, j)),
        ],
        out_shape=[jax.ShapeDtypeStruct((s, e), BF16), jax.ShapeDtypeStruct((s, e), BF16)],
        scratch_shapes=[pltpu.VMEM((tm, d), BF16)],
        compiler_params=_params("arbitrary", "arbitrary"),
        name="conf_in",
    )(x, norm_g.reshape(1, d), w_in, w_in, w_in)


def _conf_tail_kernel(x_ref, ym_ref, yh_ref, sz_ref, dww_ref, dwb_ref, lng_ref, lnb_ref, wout_ref,
                      o_ref, ybuf_ref, cbuf_ref, *, tm, rows):
    e = ym_ref.shape[1]
    halo = yh_ref[...].astype(F32)
    ybuf_ref[0:HALO_ROWS, :] = jnp.where(pl.program_id(0) > 0, halo, 0.0)
    ybuf_ref[HALO_ROWS:, :] = ym_ref[...].astype(F32)
    first = HALO_ROWS - (CONV_WIDTH - 1)

    def lane_body(g, carry):
        lanes = pl.ds(pl.multiple_of(g * LANES, LANES), LANES)
        for rc in range(tm // rows):
            acc = jnp.broadcast_to(dwb_ref[:, lanes], (rows, LANES))
            for k in range(CONV_WIDTH):
                acc = acc + ybuf_ref[pl.ds(rc * rows + first + k, rows), lanes] * dww_ref[pl.ds(k, 1), lanes]
            cbuf_ref[pl.ds(rc * rows, rows), lanes] = acc
        return carry

    lax.fori_loop(0, e // LANES, lane_body, 0)

    c = cbuf_ref[...]
    mu = jnp.mean(c, axis=-1, keepdims=True)
    xc = c - mu
    var = jnp.mean(xc * xc, axis=-1, keepdims=True)
    yn = xc * lax.rsqrt(var + EPS) * lng_ref[...] + lnb_ref[...]
    act = yn * _sigmoid(yn) * sz_ref[...].astype(F32)
    o_ref[...] = x_ref[...] + _dot(act.astype(BF16), wout_ref[...])


def _conf_tail(x, y, sz, dw_w, dw_b, ln_g, ln_b, w_out, *, tm=256, rows=64):
    s, d = x.shape
    e = y.shape[1]
    hb = tm // HALO_ROWS
    row = lambda i: (i, 0)
    const = lambda i: (0, 0)
    return pl.pallas_call(
        functools.partial(_conf_tail_kernel, tm=tm, rows=rows),
        grid=(s // tm,),
        in_specs=[
            pl.BlockSpec((tm, d), row),
            pl.BlockSpec((tm, e), row),
            pl.BlockSpec((HALO_ROWS, e), lambda i: (jnp.maximum(i * hb - 1, 0), 0)),
            pl.BlockSpec((tm, e), row),
            pl.BlockSpec((CONV_WIDTH, e), const),
            pl.BlockSpec((1, e), const),
            pl.BlockSpec((1, e), const),
            pl.BlockSpec((1, e), const),
            pl.BlockSpec((e, d), const),
        ],
        out_specs=pl.BlockSpec((tm, d), row),
        out_shape=jax.ShapeDtypeStruct((s, d), F32),
        scratch_shapes=[pltpu.VMEM((tm + HALO_ROWS, e), F32), pltpu.VMEM((tm, e), F32)],
        compiler_params=_params("arbitrary"),
        name="conf_tail",
    )(x, y, y, sz, dw_w, dw_b.reshape(1, e), ln_g.reshape(1, e), ln_b.reshape(1, e), w_out)


def _group_rms(u, g_row, scale):
    outs = []
    for c in range(u.shape[1] // HEAD_DIM):
        blk = u[:, c * HEAD_DIM:(c + 1) * HEAD_DIM]
        ms = jnp.mean(blk * blk, axis=-1, keepdims=True)
        outs.append(blk * lax.rsqrt(ms + EPS) * (g_row * scale))
    return jnp.concatenate(outs, axis=1)


def _qkv_kernel(x_ref, gkv_ref, gb_ref, kg_ref, qg_ref, wk_ref, wv_ref, wq_ref, wz_ref,
                k_ref, v_ref, q_ref, sz_ref, hkv_ref, hb_ref, *, q_scale):
    @pl.when(pl.program_id(1) == 0)
    def _():
        x = x_ref[...]
        r = x * lax.rsqrt(jnp.mean(x * x, axis=-1, keepdims=True) + EPS)
        hkv_ref[...] = (r * gkv_ref[...]).astype(BF16)
        hb_ref[...] = (r * gb_ref[...]).astype(BF16)

    hkv = hkv_ref[...]
    hb = hb_ref[...]
    k_ref[...] = _group_rms(_dot(hkv, wk_ref[...]), kg_ref[...], 1.0).astype(BF16)
    v_ref[...] = _dot(hkv, wv_ref[...]).astype(BF16)
    q_ref[...] = _group_rms(_dot(hb, wq_ref[...]), qg_ref[...], q_scale).astype(BF16)
    z = _dot(hb, wz_ref[...])
    sz_ref[...] = (z * _sigmoid(z)).astype(BF16)


def _qkv(x, kv_norm_g, b_norm_g, k_norm_g, q_norm_g, w_kv, w_b, *, q_scale, tm=512, tn=512):
    s, d = x.shape
    n = w_kv.shape[1] // 2
    nj = n // tn
    vec = lambda c: pl.BlockSpec((1, c), lambda i, j: (0, 0))
    w_lo = pl.BlockSpec((d, tn), lambda i, j: (0, j))
    w_hi = pl.BlockSpec((d, tn), lambda i, j: (0, j + nj))
    out = pl.BlockSpec((tm, tn), lambda i, j: (i, j))
    act = jax.ShapeDtypeStruct((s, n), BF16)
    return pl.pallas_call(
        functools.partial(_qkv_kernel, q_scale=q_scale),
        grid=(s // tm, nj),
        in_specs=[pl.BlockSpec((tm, d), lambda i, j: (i, 0)), vec(d), vec(d), vec(HEAD_DIM), vec(HEAD_DIM),
                  w_lo, w_hi, w_lo, w_hi],
        out_specs=[out, out, out, out],
        out_shape=[act, act, act, act],
        scratch_shapes=[pltpu.VMEM((tm, d), BF16), pltpu.VMEM((tm, d), BF16)],
        compiler_params=_params("arbitrary", "arbitrary"),
        name="qkv_proj",
    )(x, kv_norm_g.reshape(1, d), b_norm_g.reshape(1, d), k_norm_g.reshape(1, HEAD_DIM),
      q_norm_g.reshape(1, HEAD_DIM), w_kv, w_kv, w_b, w_b)


def _attn_kernel(lam_ref, q_ref, k_ref, v_ref, sz_ref, subg_ref, o_ref, acc_ref, m_ref, l_ref,
                 *, tq, tk, lam_init):
    qi = pl.program_id(1)
    acc_ref[...] = jnp.zeros_like(acc_ref)
    m_ref[...] = jnp.full_like(m_ref, NEG_BIG)
    l_ref[...] = jnp.zeros_like(l_ref)

    def step(j, masked):
        start = pl.multiple_of(j * tk, tk)
        ks = k_ref[pl.ds(start, tk), :]
        vs = v_ref[pl.ds(start, tk), :]
        if masked:
            q_chunk = (qi * tq + lax.broadcasted_iota(jnp.int32, (tq, tk), 0)) // CHUNK
            k_chunk = (j * tk + lax.broadcasted_iota(jnp.int32, (tq, tk), 1)) // CHUNK
            visible = k_chunk <= q_chunk
        for m in range(2):
            qm = q_ref[:, m * HEAD_DIM:(m + 1) * HEAD_DIM]
            km = ks[:, m * HEAD_DIM:(m + 1) * HEAD_DIM]
            sc = lax.dot_general(qm, km, (((1,), (1,)), ((), ())), preferred_element_type=F32)
            if masked:
                sc = jnp.where(visible, sc, NEG_BIG)
            m_prev = m_ref[m]
            m_next = jnp.maximum(m_prev, jnp.max(sc, axis=1, keepdims=True))
            p = jnp.exp2(sc - pltpu.repeat(m_next, tk // LANES, axis=1))
            alpha = jnp.exp2(m_prev - m_next)
            l_ref[m] = alpha * l_ref[m] + jnp.sum(p, axis=1, keepdims=True)
            m_ref[m] = m_next
            acc_ref[m] = acc_ref[m] * pltpu.repeat(alpha, V_DIM // LANES, axis=1) + _dot(p.astype(BF16), vs)

    n_full = (qi * tq) // tk

    def body(j, carry):
        step(j, False)
        return carry

    lax.fori_loop(0, n_full, body, 0)
    for d in range(tq // tk):
        step(n_full + d, True)

    lp = lam_ref[...]
    lam = (jnp.exp(jnp.sum(lp[0:1] * lp[1:2], axis=-1, keepdims=True))
           - jnp.exp(jnp.sum(lp[2:3] * lp[3:4], axis=-1, keepdims=True)) + lam_init)
    o1 = acc_ref[0] / pltpu.repeat(l_ref[0], V_DIM // LANES, axis=1)
    o2 = acc_ref[1] / pltpu.repeat(l_ref[1], V_DIM // LANES, axis=1)
    o = o1 - lam * o2
    ms = jnp.mean(o * o, axis=-1, keepdims=True)
    o = o * lax.rsqrt(ms + EPS) * subg_ref[...] * (1.0 - lam_init)
    o_ref[...] = (o * sz_ref[...].astype(F32)).astype(BF16)


def _attention(q, k, v, sz, lam_p, subln_g, *, lam_init, tq=512, tk=512):
    s, n = q.shape
    heads = n // V_DIM
    blk = pl.BlockSpec((tq, V_DIM), lambda h, i: (i, h))
    whole = pl.BlockSpec((s, V_DIM), lambda h, i: (0, h))
    return pl.pallas_call(
        functools.partial(_attn_kernel, tq=tq, tk=tk, lam_init=lam_init),
        grid=(heads, s // tq),
        in_specs=[pl.BlockSpec(lam_p.shape, lambda h, i: (0, 0)), blk, whole, whole, blk,
                  pl.BlockSpec((1, V_DIM), lambda h, i: (0, 0))],
        out_specs=blk,
        out_shape=jax.ShapeDtypeStruct((s, n), BF16),
        scratch_shapes=[pltpu.VMEM((2, tq, V_DIM), F32), pltpu.VMEM((2, tq, LANES), F32),
                        pltpu.VMEM((2, tq, LANES), F32)],
        compiler_params=_params("arbitrary", "arbitrary"),
        name="diff_attn",
    )(lam_p, q, k, v, sz, subln_g.reshape(1, V_DIM))


def _out_proj_kernel(x_ref, a_ref, w_ref, o_ref):
    o_ref[...] = x_ref[...] + _dot(a_ref[...], w_ref[...])


def _out_proj(x, a, w, *, tm=512):
    s, d = x.shape
    n = a.shape[1]
    return pl.pallas_call(
        _out_proj_kernel,
        grid=(s // tm,),
        in_specs=[pl.BlockSpec((tm, d), lambda i: (i, 0)), pl.BlockSpec((tm, n), lambda i: (i, 0)),
                  pl.BlockSpec((n, d), lambda i: (0, 0))],
        out_specs=pl.BlockSpec((tm, d), lambda i: (i, 0)),
        out_shape=jax.ShapeDtypeStruct((s, d), F32),
        compiler_params=_params("arbitrary"),
        name="out_proj",
    )(x, a, w)


def kernel(x, a_norm_g, a_w_in, a_dw_w, a_dw_b, a_ln_g, a_ln_b, a_w_out, kv_norm_g, w_kv, k_norm_g,
           b_norm_g, b_w_in, b_q_norm_g, b_lambda, b_subln_g, b_w_out):
    bsz, s, d = x.shape
    n_a, n_b = a_norm_g.shape[0], b_norm_g.shape[0]
    assert bsz == 1 and n_a == 1 and n_b == 1, "one conformer layer followed by one attention layer"
    xs = x.reshape(s, d)

    y, sz = _conf_in(xs, a_norm_g[0], a_w_in[0].astype(BF16))
    xs = _conf_tail(xs, y, sz, a_dw_w[0], a_dw_b[0], a_ln_g[0], a_ln_b[0], a_w_out[0].astype(BF16))

    layer_idx = n_a + 1
    lam_init = 0.8 - 0.6 * math.exp(-0.3 * (layer_idx - 1))
    q_scale = HEAD_DIM ** -0.5 * LOG2E
    k, v, q, sz = _qkv(xs, kv_norm_g, b_norm_g[0], k_norm_g, b_q_norm_g[0], w_kv.astype(BF16),
                       b_w_in[0].astype(BF16), q_scale=q_scale)
    og = _attention(q, k, v, sz, b_lambda[0], b_subln_g[0], lam_init=lam_init)
    out = _out_proj(xs, og, b_w_out[0].astype(BF16))
    return out.reshape(bsz, s, d)
```

```python
import functools
import math

import jax
import jax.numpy as jnp
from jax import lax
from jax.experimental import pallas as pl
from jax.experimental.pallas import tpu as pltpu

EPS = 1e-6
CHUNK = 64
CONV_WIDTH = 31
HEAD_DIM = 128
V_DIM = 2 * HEAD_DIM
LANES = 128
HALO_ROWS = 32
VMEM_LIMIT = 56 * 1024 * 1024
LOG2E = 1.4426950408889634
NEG_BIG = -1e30
SCORE_BOUND = 64.0
BF16_MARGIN = 1.02

BF16 = jnp.bfloat16
F32 = jnp.float32


def _sigmoid(v):
    return 1.0 / (1.0 + jnp.exp(-v))


def _dot(a, b):
    return jnp.dot(a, b, preferred_element_type=F32)


def _dot_nt(a, b):
    return lax.dot_general(a, b, (((1,), (1,)), ((), ())), preferred_element_type=F32)


def _params(*sem):
    return pltpu.CompilerParams(dimension_semantics=sem, vmem_limit_bytes=VMEM_LIMIT)


def _conf_in_kernel(x_ref, g_ref, wa_ref, wb_ref, wz_ref, y_ref, sz_ref, h_ref):
    @pl.when(pl.program_id(1) == 0)
    def _():
        x = x_ref[...]
        ms = jnp.mean(x * x, axis=-1, keepdims=True)
        h_ref[...] = (x * lax.rsqrt(ms + EPS) * g_ref[...]).astype(BF16)

    h = h_ref[...]
    a = _dot(h, wa_ref[...])
    b = _dot(h, wb_ref[...])
    z = _dot(h, wz_ref[...])
    y_ref[...] = (a * _sigmoid(b)).astype(BF16)
    sz_ref[...] = (z * _sigmoid(z)).astype(BF16)


def _conf_in(x, norm_g, w_in, *, tm=512, tn=512):
    s, d = x.shape
    e = w_in.shape[1] // 3
    nj = e // tn
    return pl.pallas_call(
        _conf_in_kernel,
        grid=(s // tm, nj),
        in_specs=[
            pl.BlockSpec((tm, d), lambda i, j: (i, 0)),
            pl.BlockSpec((1, d), lambda i, j: (0, 0)),
            pl.BlockSpec((d, tn), lambda i, j: (0, j)),
            pl.BlockSpec((d, tn), lambda i, j: (0, j + nj)),
            pl.BlockSpec((d, tn), lambda i, j: (0, j + 2 * nj)),
        ],
        out_specs=[
            pl.BlockSpec((tm, tn), lambda i, j: (i, j)),
            pl.BlockSpec((tm, tn), lambda i, j: (i, j)),
        ],
        out_shape=[jax.ShapeDtypeStruct((s, e), BF16), jax.ShapeDtypeStruct((s, e), BF16)],
        scratch_shapes=[pltpu.VMEM((tm, d), BF16)],
        compiler_params=_params("arbitrary", "arbitrary"),
        name="conf_in",
    )(x, norm_g.reshape(1, d), w_in, w_in, w_in)


def _conf_tail_kernel(x_ref, ym_ref, yh_ref, sz_ref, dww_ref, dwb_ref, lng_ref, lnb_ref, wout_ref,
                      o_ref, ybuf_ref, cbuf_ref, *, tm, rows):
    e = ym_ref.shape[1]
    halo = yh_ref[...].astype(F32)
    ybuf_ref[0:HALO_ROWS, :] = jnp.where(pl.program_id(0) > 0, halo, 0.0)
    ybuf_ref[HALO_ROWS:, :] = ym_ref[...].astype(F32)
    first = HALO_ROWS - (CONV_WIDTH - 1)

    def lane_body(g, carry):
        lanes = pl.ds(pl.multiple_of(g * LANES, LANES), LANES)
        for rc in range(tm // rows):
            acc = jnp.broadcast_to(dwb_ref[:, lanes], (rows, LANES))
            for k in range(CONV_WIDTH):
                acc = acc + ybuf_ref[pl.ds(rc * rows + first + k, rows), lanes] * dww_ref[pl.ds(k, 1), lanes]
            cbuf_ref[pl.ds(rc * rows, rows), lanes] = acc
        return carry

    lax.fori_loop(0, e // LANES, lane_body, 0)

    c = cbuf_ref[...]
    mu = jnp.mean(c, axis=-1, keepdims=True)
    xc = c - mu
    var = jnp.mean(xc * xc, axis=-1, keepdims=True)
    yn = xc * lax.rsqrt(var + EPS) * lng_ref[...] + lnb_ref[...]
    act = yn * _sigmoid(yn) * sz_ref[...].astype(F32)
    o_ref[...] = x_ref[...] + _dot(act.astype(BF16), wout_ref[...])


def _conf_tail(x, y, sz, dw_w, dw_b, ln_g, ln_b, w_out, *, tm=256, rows=64):
    s, d = x.shape
    e = y.shape[1]
    hb = tm // HALO_ROWS
    row = lambda i: (i, 0)
    const = lambda i: (0, 0)
    return pl.pallas_call(
        functools.partial(_conf_tail_kernel, tm=tm, rows=rows),
        grid=(s // tm,),
        in_specs=[
            pl.BlockSpec((tm, d), row),
            pl.BlockSpec((tm, e), row),
            pl.BlockSpec((HALO_ROWS, e), lambda i: (jnp.maximum(i * hb - 1, 0), 0)),
            pl.BlockSpec((tm, e), row),
            pl.BlockSpec((CONV_WIDTH, e), const),
            pl.BlockSpec((1, e), const),
            pl.BlockSpec((1, e), const),
            pl.BlockSpec((1, e), const),
            pl.BlockSpec((e, d), const),
        ],
        out_specs=pl.BlockSpec((tm, d), row),
        out_shape=jax.ShapeDtypeStruct((s, d), F32),
        scratch_shapes=[pltpu.VMEM((tm + HALO_ROWS, e), F32), pltpu.VMEM((tm, e), F32)],
        compiler_params=_params("arbitrary"),
        name="conf_tail",
    )(x, y, y, sz, dw_w, dw_b.reshape(1, e), ln_g.reshape(1, e), ln_b.reshape(1, e), w_out)


def _group_rms(u, g_row, scale):
    outs = []
    for c in range(u.shape[1] // HEAD_DIM):
        blk = u[:, c * HEAD_DIM:(c + 1) * HEAD_DIM]
        ms = jnp.mean(blk * blk, axis=-1, keepdims=True)
        outs.append(blk * lax.rsqrt(ms + EPS) * (g_row * scale))
    return jnp.concatenate(outs, axis=1)


def _qkv_kernel(x_ref, gkv_ref, gb_ref, kg_ref, qg_ref, wk_ref, wv_ref, wq_ref, wz_ref,
                k_ref, v_ref, q_ref, sz_ref, hkv_ref, hb_ref, *, q_scale):
    @pl.when(pl.program_id(1) == 0)
    def _():
        x = x_ref[...]
        r = x * lax.rsqrt(jnp.mean(x * x, axis=-1, keepdims=True) + EPS)
        hkv_ref[...] = (r * gkv_ref[...]).astype(BF16)
        hb_ref[...] = (r * gb_ref[...]).astype(BF16)

    hkv = hkv_ref[...]
    hb = hb_ref[...]
    k_ref[...] = _group_rms(_dot(hkv, wk_ref[...]), kg_ref[...], 1.0).astype(BF16)
    v_ref[...] = _dot(hkv, wv_ref[...]).astype(BF16)
    q_ref[...] = _group_rms(_dot(hb, wq_ref[...]), qg_ref[...], q_scale).astype(BF16)
    z = _dot(hb, wz_ref[...])
    sz_ref[...] = (z * _sigmoid(z)).astype(BF16)


def _qkv(x, kv_norm_g, b_norm_g, k_norm_g, q_norm_g, w_kv, w_b, *, q_scale, tm=512, tn=512):
    s, d = x.shape
    n = w_kv.shape[1] // 2
    nj = n // tn
    vec = lambda c: pl.BlockSpec((1, c), lambda i, j: (0, 0))
    w_lo = pl.BlockSpec((d, tn), lambda i, j: (0, j))
    w_hi = pl.BlockSpec((d, tn), lambda i, j: (0, j + nj))
    out = pl.BlockSpec((tm, tn), lambda i, j: (i, j))
    act = jax.ShapeDtypeStruct((s, n), BF16)
    return pl.pallas_call(
        functools.partial(_qkv_kernel, q_scale=q_scale),
        grid=(s // tm, nj),
        in_specs=[pl.BlockSpec((tm, d), lambda i, j: (i, 0)), vec(d), vec(d), vec(HEAD_DIM), vec(HEAD_DIM),
                  w_lo, w_hi, w_lo, w_hi],
        out_specs=[out, out, out, out],
        out_shape=[act, act, act, act],
        scratch_shapes=[pltpu.VMEM((tm, d), BF16), pltpu.VMEM((tm, d), BF16)],
        compiler_params=_params("arbitrary", "arbitrary"),
        name="qkv_proj",
    )(x, kv_norm_g.reshape(1, d), b_norm_g.reshape(1, d), k_norm_g.reshape(1, HEAD_DIM),
      q_norm_g.reshape(1, HEAD_DIM), w_kv, w_kv, w_b, w_b)


def _visible(qi, j, tq, tk):
    q_chunk = (qi * tq + lax.broadcasted_iota(jnp.int32, (tq, tk), 0)) // CHUNK
    k_chunk = (j * tk + lax.broadcasted_iota(jnp.int32, (tq, tk), 1)) // CHUNK
    return k_chunk <= q_chunk


def _attn_finish(lam_ref, sz_ref, subg_ref, o_ref, o_maps, lam_init):
    lp = lam_ref[...]
    lam = (jnp.exp(jnp.sum(lp[0:1] * lp[1:2], axis=-1, keepdims=True))
           - jnp.exp(jnp.sum(lp[2:3] * lp[3:4], axis=-1, keepdims=True)) + lam_init)
    o = o_maps[0] - lam * o_maps[1]
    ms = jnp.mean(o * o, axis=-1, keepdims=True)
    o = o * lax.rsqrt(ms + EPS) * subg_ref[...] * (1.0 - lam_init)
    o_ref[...] = (o * sz_ref[...].astype(F32)).astype(BF16)


def _attn_online_kernel(lam_ref, q_ref, k_ref, v_ref, sz_ref, subg_ref, o_ref, acc_ref, m_ref, l_ref,
                        *, tq, tk, lam_init):
    qi = pl.program_id(1)
    acc_ref[...] = jnp.zeros_like(acc_ref)
    l_ref[...] = jnp.zeros_like(l_ref)
    m_ref[...] = jnp.full_like(m_ref, NEG_BIG)

    def step(j, masked):
        start = pl.multiple_of(j * tk, tk)
        ks = k_ref[pl.ds(start, tk), :]
        vs = v_ref[pl.ds(start, tk), :]
        for m in range(2):
            sc = _dot_nt(q_ref[:, m * HEAD_DIM:(m + 1) * HEAD_DIM], ks[:, m * HEAD_DIM:(m + 1) * HEAD_DIM])
            if masked:
                sc = jnp.where(_visible(qi, j, tq, tk), sc, NEG_BIG)
            m_prev = m_ref[m]
            m_next = jnp.maximum(m_prev, jnp.max(sc, axis=1, keepdims=True))
            p = jnp.exp2(sc - pltpu.repeat(m_next, tk // LANES, axis=1))
            alpha = jnp.exp2(m_prev - m_next)
            l_ref[m] = alpha * l_ref[m] + jnp.sum(p, axis=1, keepdims=True)
            m_ref[m] = m_next
            acc_ref[m] = acc_ref[m] * pltpu.repeat(alpha, V_DIM // LANES, axis=1) + _dot(p.astype(BF16), vs)

    n_full = (qi * tq) // tk

    def body(j, carry):
        step(j, False)
        return carry

    lax.fori_loop(0, n_full, body, 0)
    for d in range(tq // tk):
        step(n_full + d, True)

    o_maps = [acc_ref[m] / l_ref[m][:, 0:1] for m in range(2)]
    _attn_finish(lam_ref, sz_ref, subg_ref, o_ref, o_maps, lam_init)


def _attn_bounded_kernel(lam_ref, q_ref, k_ref, v_ref, sz_ref, subg_ref, o_ref, acc_ref, l_ref,
                         sa_ref, sb_ref, *, t, lam_init):
    qi = pl.program_id(1)
    acc_ref[...] = jnp.zeros_like(acc_ref)
    l_ref[...] = jnp.zeros_like(l_ref)

    def scores(j, s_ref):
        ks = k_ref[pl.ds(pl.multiple_of(j * t, t), t), :]
        for m in range(2):
            s_ref[m] = _dot_nt(q_ref[:, m * HEAD_DIM:(m + 1) * HEAD_DIM], ks[:, m * HEAD_DIM:(m + 1) * HEAD_DIM])

    def consume(j, s_ref, masked):
        vs = v_ref[pl.ds(pl.multiple_of(j * t, t), t), :]
        for m in range(2):
            sc = s_ref[m]
            if masked:
                sc = jnp.where(_visible(qi, j, t, t), sc, NEG_BIG)
            p = jnp.exp2(sc)
            part = p[:, 0:LANES]
            for c in range(1, t // LANES):
                part = part + p[:, c * LANES:(c + 1) * LANES]
            l_ref[m] += part
            acc_ref[m] += _dot(p.astype(BF16), vs)

    scores(0, sa_ref)

    def pair(jj, carry):
        j = 2 * jj
        scores(j + 1, sb_ref)
        consume(j, sa_ref, False)
        scores(j + 2, sa_ref)
        consume(j + 1, sb_ref, False)
        return carry

    lax.fori_loop(0, qi // 2, pair, 0)

    @pl.when(qi % 2 == 1)
    def _():
        scores(qi, sb_ref)
        consume(qi - 1, sa_ref, False)
        consume(qi, sb_ref, True)

    @pl.when(qi % 2 == 0)
    def _():
        consume(qi, sa_ref, True)

    o_maps = [acc_ref[m] / jnp.sum(l_ref[m], axis=1, keepdims=True) for m in range(2)]
    _attn_finish(lam_ref, sz_ref, subg_ref, o_ref, o_maps, lam_init)


def _attention(q, k, v, sz, lam_p, subln_g, *, lam_init, bounded, t=512):
    s, n = q.shape
    heads = n // V_DIM
    blk = pl.BlockSpec((t, V_DIM), lambda h, i: (i, h))
    whole = pl.BlockSpec((s, V_DIM), lambda h, i: (0, h))
    stats = pltpu.VMEM((2, t, LANES), F32)
    if bounded:
        body = functools.partial(_attn_bounded_kernel, t=t, lam_init=lam_init)
        scratch = [pltpu.VMEM((2, t, V_DIM), F32), stats, pltpu.VMEM((2, t, t), F32), pltpu.VMEM((2, t, t), F32)]
    else:
        body = functools.partial(_attn_online_kernel, tq=t, tk=t, lam_init=lam_init)
        scratch = [pltpu.VMEM((2, t, V_DIM), F32), stats, stats]
    return pl.pallas_call(
        body,
        grid=(heads, s // t),
        in_specs=[pl.BlockSpec(lam_p.shape, lambda h, i: (0, 0)), blk, whole, whole, blk,
                  pl.BlockSpec((1, V_DIM), lambda h, i: (0, 0))],
        out_specs=blk,
        out_shape=jax.ShapeDtypeStruct((s, n), BF16),
        scratch_shapes=scratch,
        compiler_params=_params("arbitrary", "arbitrary"),
        name="diff_attn_bounded" if bounded else "diff_attn_online",
    )(lam_p, q, k, v, sz, subln_g.reshape(1, V_DIM))


def _out_proj_kernel(x_ref, a_ref, w_ref, o_ref):
    o_ref[...] = x_ref[...] + _dot(a_ref[...], w_ref[...])


def _out_proj(x, a, w, *, tm=512):
    s, d = x.shape
    n = a.shape[1]
    return pl.pallas_call(
        _out_proj_kernel,
        grid=(s // tm,),
        in_specs=[pl.BlockSpec((tm, d), lambda i: (i, 0)), pl.BlockSpec((tm, n), lambda i: (i, 0)),
                  pl.BlockSpec((n, d), lambda i: (0, 0))],
        out_specs=pl.BlockSpec((tm, d), lambda i: (i, 0)),
        out_shape=jax.ShapeDtypeStruct((s, d), F32),
        compiler_params=_params("arbitrary"),
        name="out_proj",
    )(x, a, w)


def kernel(x, a_norm_g, a_w_in, a_dw_w, a_dw_b, a_ln_g, a_ln_b, a_w_out, kv_norm_g, w_kv, k_norm_g,
           b_norm_g, b_w_in, b_q_norm_g, b_lambda, b_subln_g, b_w_out):
    bsz, s, d = x.shape
    n_a, n_b = a_norm_g.shape[0], b_norm_g.shape[0]
    assert bsz == 1 and n_a == 1 and n_b == 1, "one conformer layer followed by one attention layer"
    xs = x.reshape(s, d)

    y, sz = _conf_in(xs, a_norm_g[0], a_w_in[0].astype(BF16))
    xs = _conf_tail(xs, y, sz, a_dw_w[0], a_dw_b[0], a_ln_g[0], a_ln_b[0], a_w_out[0].astype(BF16))

    layer_idx = n_a + 1
    lam_init = 0.8 - 0.6 * math.exp(-0.3 * (layer_idx - 1))
    q_scale = HEAD_DIM ** -0.5 * LOG2E
    k, v, q, sz = _qkv(xs, kv_norm_g, b_norm_g[0], k_norm_g, b_q_norm_g[0], w_kv.astype(BF16),
                       b_w_in[0].astype(BF16), q_scale=q_scale)
    score_bound = (HEAD_DIM ** 0.5 * LOG2E * BF16_MARGIN
                   * jnp.max(jnp.abs(b_q_norm_g[0])) * jnp.max(jnp.abs(k_norm_g)))
    attend = functools.partial(_attention, lam_init=lam_init)
    og = lax.cond(score_bound <= SCORE_BOUND,
                  functools.partial(attend, bounded=True), functools.partial(attend, bounded=False),
                  q, k, v, sz, b_lambda[0], b_subln_g[0])
    out = _out_proj(xs, og, b_w_out[0].astype(BF16))
    return out.reshape(bsz, s, d)
```

```python
import functools
import math

import jax
import jax.numpy as jnp
from jax import lax
from jax.experimental import pallas as pl
from jax.experimental.pallas import tpu as pltpu

EPS = 1e-6
CHUNK = 64
CONV_WIDTH = 31
HEAD_DIM = 128
V_DIM = 2 * HEAD_DIM
LANES = 128
SUBLANES = 8
HALO_ROWS = 32
VMEM_LIMIT = 56 * 1024 * 1024
LOG2E = 1.4426950408889634
NEG_BIG = -1e30
SCORE_BOUND = 64.0
BF16_MARGIN = 1.02

BF16 = jnp.bfloat16
F32 = jnp.float32


def _sigmoid(v):
    return 1.0 / (1.0 + jnp.exp(-v))


def _dot(a, b):
    return jnp.dot(a, b, preferred_element_type=F32)


def _lane_tile(a, n):
    return jnp.concatenate([a] * n, axis=1)


def _params(*sem):
    return pltpu.CompilerParams(dimension_semantics=sem, vmem_limit_bytes=VMEM_LIMIT)


def _conf_in_kernel(x_ref, g_ref, wa_ref, wb_ref, wz_ref, y_ref, sz_ref, h_ref):
    @pl.when(pl.program_id(1) == 0)
    def _():
        x = x_ref[...]
        ms = jnp.mean(x * x, axis=-1, keepdims=True)
        h_ref[...] = (x * lax.rsqrt(ms + EPS) * g_ref[...]).astype(BF16)

    h = h_ref[...]
    a = _dot(h, wa_ref[...])
    b = _dot(h, wb_ref[...])
    z = _dot(h, wz_ref[...])
    y_ref[...] = (a * _sigmoid(b)).astype(BF16)
    sz_ref[...] = (z * _sigmoid(z)).astype(BF16)


def _conf_in(x, norm_g, w_in, *, tm=512, tn=1024):
    s, d = x.shape
    e = w_in.shape[1] // 3
    nj = e // tn
    return pl.pallas_call(
        _conf_in_kernel,
        grid=(s // tm, nj),
        in_specs=[
            pl.BlockSpec((tm, d), lambda i, j: (i, 0)),
            pl.BlockSpec((1, d), lambda i, j: (0, 0)),
            pl.BlockSpec((d, tn), lambda i, j: (0, j)),
            pl.BlockSpec((d, tn), lambda i, j: (0, j + nj)),
            pl.BlockSpec((d, tn), lambda i, j: (0, j + 2 * nj)),
        ],
        out_specs=[
            pl.BlockSpec((tm, tn), lambda i, j: (i, j)),
            pl.BlockSpec((tm, tn), lambda i, j: (i, j)),
        ],
        out_shape=[jax.ShapeDtypeStruct((s, e), BF16), jax.ShapeDtypeStruct((s, e), BF16)],
        scratch_shapes=[pltpu.VMEM((tm, d), BF16)],
        compiler_params=_params("arbitrary", "arbitrary"),
        name="conf_in",
    )(x, norm_g.reshape(1, d), w_in, w_in, w_in)


def _conf_tail_kernel(x_ref, ym_ref, yh_ref, sz_ref, dww_ref, dwb_ref, lng_ref, lnb_ref, wout_ref,
                      o_ref, ybuf_ref, cbuf_ref, *, tm, rows):
    e = ym_ref.shape[1]
    halo = yh_ref[...].astype(F32)
    ybuf_ref[0:HALO_ROWS, :] = jnp.where(pl.program_id(0) > 0, halo, 0.0)
    ybuf_ref[HALO_ROWS:, :] = ym_ref[...].astype(F32)
    first = HALO_ROWS - (CONV_WIDTH - 1)

    def lane_body(g, carry):
        lanes = pl.ds(pl.multiple_of(g * LANES, LANES), LANES)
        for rc in range(tm // rows):
            acc = jnp.broadcast_to(dwb_ref[:, lanes], (rows, LANES))
            for r in range(SUBLANES):
                span = rows + SUBLANES if r else rows
                z = None
                for k in range(CONV_WIDTH):
                    if (first + k) % SUBLANES != r:
                        continue
                    aligned = rc * rows + first + k - r
                    term = ybuf_ref[pl.ds(aligned, span), lanes] * dww_ref[pl.ds(k, 1), lanes]
                    z = term if z is None else z + term
                acc = acc + z[r:r + rows]
            cbuf_ref[pl.ds(rc * rows, rows), lanes] = acc
        return carry

    lax.fori_loop(0, e // LANES, lane_body, 0)

    c = cbuf_ref[...]
    mu = jnp.mean(c, axis=-1, keepdims=True)
    xc = c - mu
    var = jnp.mean(xc * xc, axis=-1, keepdims=True)
    yn = xc * lax.rsqrt(var + EPS) * lng_ref[...] + lnb_ref[...]
    act = yn * _sigmoid(yn) * sz_ref[...].astype(F32)
    o_ref[...] = x_ref[...] + _dot(act.astype(BF16), wout_ref[...])


def _conf_tail(x, y, sz, dw_w, dw_b, ln_g, ln_b, w_out, *, tm=256, rows=64):
    s, d = x.shape
    e = y.shape[1]
    hb = tm // HALO_ROWS
    row = lambda i: (i, 0)
    const = lambda i: (0, 0)
    return pl.pallas_call(
        functools.partial(_conf_tail_kernel, tm=tm, rows=rows),
        grid=(s // tm,),
        in_specs=[
            pl.BlockSpec((tm, d), row),
            pl.BlockSpec((tm, e), row),
            pl.BlockSpec((HALO_ROWS, e), lambda i: (jnp.maximum(i * hb - 1, 0), 0)),
            pl.BlockSpec((tm, e), row),
            pl.BlockSpec((CONV_WIDTH, e), const),
            pl.BlockSpec((1, e), const),
            pl.BlockSpec((1, e), const),
            pl.BlockSpec((1, e), const),
            pl.BlockSpec((e, d), const),
        ],
        out_specs=pl.BlockSpec((tm, d), row),
        out_shape=jax.ShapeDtypeStruct((s, d), F32),
        scratch_shapes=[pltpu.VMEM((tm + HALO_ROWS, e), F32), pltpu.VMEM((tm, e), F32)],
        compiler_params=_params("arbitrary"),
        name="conf_tail",
    )(x, y, y, sz, dw_w, dw_b.reshape(1, e), ln_g.reshape(1, e), ln_b.reshape(1, e), w_out)


def _group_rms(u, g_row, scale):
    outs = []
    for c in range(u.shape[1] // HEAD_DIM):
        blk = u[:, c * HEAD_DIM:(c + 1) * HEAD_DIM]
        ms = jnp.mean(blk * blk, axis=-1, keepdims=True)
        outs.append(blk * lax.rsqrt(ms + EPS) * (g_row * scale))
    return jnp.concatenate(outs, axis=1)


def _qkv_kernel(x_ref, gkv_ref, gb_ref, kg_ref, qg_ref, wk_ref, wv_ref, wq_ref, wz_ref,
                kt_ref, v_ref, q_ref, sz_ref, hkv_ref, hb_ref, *, q_scale):
    @pl.when(pl.program_id(1) == 0)
    def _():
        x = x_ref[...]
        r = x * lax.rsqrt(jnp.mean(x * x, axis=-1, keepdims=True) + EPS)
        hkv_ref[...] = (r * gkv_ref[...]).astype(BF16)
        hb_ref[...] = (r * gb_ref[...]).astype(BF16)

    hkv = hkv_ref[...]
    hb = hb_ref[...]
    kt_ref[...] = _group_rms(_dot(hkv, wk_ref[...]), kg_ref[...], 1.0).T.astype(BF16)
    v_ref[...] = _dot(hkv, wv_ref[...]).astype(BF16)
    q_ref[...] = _group_rms(_dot(hb, wq_ref[...]), qg_ref[...], q_scale).astype(BF16)
    z = _dot(hb, wz_ref[...])
    sz_ref[...] = (z * _sigmoid(z)).astype(BF16)


def _qkv(x, kv_norm_g, b_norm_g, k_norm_g, q_norm_g, w_kv, w_b, *, q_scale, tm=512, tn=512):
    s, d = x.shape
    n = w_kv.shape[1] // 2
    nj = n // tn
    vec = lambda c: pl.BlockSpec((1, c), lambda i, j: (0, 0))
    w_lo = pl.BlockSpec((d, tn), lambda i, j: (0, j))
    w_hi = pl.BlockSpec((d, tn), lambda i, j: (0, j + nj))
    out = pl.BlockSpec((tm, tn), lambda i, j: (i, j))
    act = jax.ShapeDtypeStruct((s, n), BF16)
    return pl.pallas_call(
        functools.partial(_qkv_kernel, q_scale=q_scale),
        grid=(s // tm, nj),
        in_specs=[pl.BlockSpec((tm, d), lambda i, j: (i, 0)), vec(d), vec(d), vec(HEAD_DIM), vec(HEAD_DIM),
                  w_lo, w_hi, w_lo, w_hi],
        out_specs=[pl.BlockSpec((tn, tm), lambda i, j: (j, i)), out, out, out],
        out_shape=[jax.ShapeDtypeStruct((n, s), BF16), act, act, act],
        scratch_shapes=[pltpu.VMEM((tm, d), BF16), pltpu.VMEM((tm, d), BF16)],
        compiler_params=_params("arbitrary", "arbitrary"),
        name="qkv_proj",
    )(x, kv_norm_g.reshape(1, d), b_norm_g.reshape(1, d), k_norm_g.reshape(1, HEAD_DIM),
      q_norm_g.reshape(1, HEAD_DIM), w_kv, w_kv, w_b, w_b)


def _visible(rows, cols):
    shift = CHUNK.bit_length() - 1
    q_chunk = lax.broadcasted_iota(jnp.int32, (rows, cols), 0) >> shift
    k_chunk = lax.broadcasted_iota(jnp.int32, (rows, cols), 1) >> shift
    return k_chunk <= q_chunk


def _attn_finish(lam_ref, sz_ref, subg_ref, o_ref, o_maps, lam_init):
    lp = lam_ref[...]
    lam = (jnp.exp(jnp.sum(lp[0:1] * lp[1:2], axis=-1, keepdims=True))
           - jnp.exp(jnp.sum(lp[2:3] * lp[3:4], axis=-1, keepdims=True)) + lam_init)
    o = o_maps[0] - lam * o_maps[1]
    ms = jnp.mean(o * o, axis=-1, keepdims=True)
    o = o * lax.rsqrt(ms + EPS) * subg_ref[...] * (1.0 - lam_init)
    o_ref[...] = (o * sz_ref[...].astype(F32)).astype(BF16)


def _map_cols(m):
    return slice(m * HEAD_DIM, (m + 1) * HEAD_DIM)


def _attn_online_kernel(lam_ref, q_ref, kt_ref, v_ref, sz_ref, subg_ref, o_ref, acc_ref, m_ref, l_ref,
                        *, t, lam_init):
    qi = pl.program_id(1)
    acc_ref[...] = jnp.zeros_like(acc_ref)
    l_ref[...] = jnp.zeros_like(l_ref)
    m_ref[...] = jnp.full_like(m_ref, NEG_BIG)

    def step(j, masked):
        cols = pl.ds(pl.multiple_of(j * t, t), t)
        kts = kt_ref[:, cols]
        vs = v_ref[cols, :]
        for m in range(2):
            sc = _dot(q_ref[:, _map_cols(m)], kts[_map_cols(m), :])
            if masked:
                sc = jnp.where(_visible(t, t), sc, NEG_BIG)
            m_prev = m_ref[m]
            m_next = jnp.maximum(m_prev, jnp.max(sc, axis=1, keepdims=True))
            p = jnp.exp2(sc - _lane_tile(m_next, t // LANES))
            alpha = jnp.exp2(m_prev - m_next)
            l_ref[m] = alpha * l_ref[m] + jnp.sum(p, axis=1, keepdims=True)
            m_ref[m] = m_next
            acc_ref[m] = acc_ref[m] * _lane_tile(alpha, V_DIM // LANES) + _dot(p.astype(BF16), vs)

    def body(j, carry):
        step(j, False)
        return carry

    lax.fori_loop(0, qi, body, 0)
    step(qi, True)

    o_maps = [acc_ref[m] / l_ref[m][:, 0:1] for m in range(2)]
    _attn_finish(lam_ref, sz_ref, subg_ref, o_ref, o_maps, lam_init)


def _attn_bounded_kernel(lam_ref, q_ref, kt_ref, v_ref, sz_ref, subg_ref, o_ref, acc_ref, l_ref,
                         sa_ref, sb_ref, *, tq, tk, lam_init):
    qi = pl.program_id(1)
    ratio = tq // tk
    assert ratio % 2 == 0, "the pair loop covers the qi * ratio fully visible kv blocks"
    acc_ref[...] = jnp.zeros_like(acc_ref)
    l_ref[...] = jnp.zeros_like(l_ref)

    def scores(j, s_ref, row0=0):
        kts = kt_ref[:, pl.ds(pl.multiple_of(j * tk, tk), tk)]
        for m in range(2):
            s_ref[m, row0:, :] = _dot(q_ref[row0:, _map_cols(m)], kts[_map_cols(m), :])

    def consume(j, s_ref, diag_row0=None):
        row0 = 0 if diag_row0 is None else diag_row0
        vs = v_ref[pl.ds(pl.multiple_of(j * tk, tk), tk), :]
        for m in range(2):
            sc = s_ref[m, row0:, :]
            if diag_row0 is not None:
                sc = jnp.where(_visible(tq - row0, tk), sc, NEG_BIG)
            p = jnp.exp2(sc)
            part = p[:, 0:LANES]
            for c in range(1, tk // LANES):
                part = part + p[:, c * LANES:(c + 1) * LANES]
            l_ref[m, row0:, :] += part
            acc_ref[m, row0:, :] += _dot(p.astype(BF16), vs)

    n_full = qi * ratio
    scores(0, sa_ref)

    def pair(jj, carry):
        j = 2 * jj
        scores(j + 1, sb_ref)
        consume(j, sa_ref)
        scores(j + 2, sa_ref)
        consume(j + 1, sb_ref)
        return carry

    lax.fori_loop(0, n_full // 2, pair, 0)

    bufs = (sa_ref, sb_ref)
    for d in range(ratio):
        if d + 1 < ratio:
            scores(n_full + d + 1, bufs[(d + 1) % 2], row0=(d + 1) * tk)
        consume(n_full + d, bufs[d % 2], diag_row0=d * tk)

    o_maps = [acc_ref[m] / jnp.sum(l_ref[m], axis=1, keepdims=True) for m in range(2)]
    _attn_finish(lam_ref, sz_ref, subg_ref, o_ref, o_maps, lam_init)


def _attention(q, kt, v, sz, lam_p, subln_g, *, lam_init, bounded, tq=1024, tk=512):
    s, n = q.shape
    heads = n // V_DIM
    if bounded:
        body = functools.partial(_attn_bounded_kernel, tq=tq, tk=tk, lam_init=lam_init)
        scores = pltpu.VMEM((2, tq, tk), F32)
        scratch = [pltpu.VMEM((2, tq, V_DIM), F32), pltpu.VMEM((2, tq, LANES), F32), scores, scores]
    else:
        tq = tk
        body = functools.partial(_attn_online_kernel, t=tk, lam_init=lam_init)
        stats = pltpu.VMEM((2, tq, LANES), F32)
        scratch = [pltpu.VMEM((2, tq, V_DIM), F32), stats, stats]
    blk = pl.BlockSpec((tq, V_DIM), lambda h, i: (i, h))
    return pl.pallas_call(
        body,
        grid=(heads, s // tq),
        in_specs=[pl.BlockSpec(lam_p.shape, lambda h, i: (0, 0)), blk,
                  pl.BlockSpec((V_DIM, s), lambda h, i: (h, 0)), pl.BlockSpec((s, V_DIM), lambda h, i: (0, h)), blk,
                  pl.BlockSpec((1, V_DIM), lambda h, i: (0, 0))],
        out_specs=blk,
        out_shape=jax.ShapeDtypeStruct((s, n), BF16),
        scratch_shapes=scratch,
        compiler_params=_params("arbitrary", "arbitrary"),
        name="diff_attn_bounded" if bounded else "diff_attn_online",
    )(lam_p, q, kt, v, sz, subln_g.reshape(1, V_DIM))


def _out_proj_kernel(x_ref, a_ref, w_ref, o_ref):
    o_ref[...] = x_ref[...] + _dot(a_ref[...], w_ref[...])


def _out_proj(x, a, w, *, tm=512):
    s, d = x.shape
    n = a.shape[1]
    return pl.pallas_call(
        _out_proj_kernel,
        grid=(s // tm,),
        in_specs=[pl.BlockSpec((tm, d), lambda i: (i, 0)), pl.BlockSpec((tm, n), lambda i: (i, 0)),
                  pl.BlockSpec((n, d), lambda i: (0, 0))],
        out_specs=pl.BlockSpec((tm, d), lambda i: (i, 0)),
        out_shape=jax.ShapeDtypeStruct((s, d), F32),
        compiler_params=_params("arbitrary"),
        name="out_proj",
    )(x, a, w)


def kernel(x, a_norm_g, a_w_in, a_dw_w, a_dw_b, a_ln_g, a_ln_b, a_w_out, kv_norm_g, w_kv, k_norm_g,
           b_norm_g, b_w_in, b_q_norm_g, b_lambda, b_subln_g, b_w_out):
    bsz, s, d = x.shape
    n_a, n_b = a_norm_g.shape[0], b_norm_g.shape[0]
    assert bsz == 1 and n_a == 1 and n_b == 1, "one conformer layer followed by one attention layer"
    xs = x.reshape(s, d)

    y, sz = _conf_in(xs, a_norm_g[0], a_w_in[0].astype(BF16))
    xs = _conf_tail(xs, y, sz, a_dw_w[0], a_dw_b[0], a_ln_g[0], a_ln_b[0], a_w_out[0].astype(BF16))

    layer_idx = n_a + 1
    lam_init = 0.8 - 0.6 * math.exp(-0.3 * (layer_idx - 1))
    q_scale = HEAD_DIM ** -0.5 * LOG2E
    kt, v, q, sz = _qkv(xs, kv_norm_g, b_norm_g[0], k_norm_g, b_q_norm_g[0], w_kv.astype(BF16),
                       b_w_in[0].astype(BF16), q_scale=q_scale)
    score_bound = (HEAD_DIM ** 0.5 * LOG2E * BF16_MARGIN
                   * jnp.max(jnp.abs(b_q_norm_g[0])) * jnp.max(jnp.abs(k_norm_g)))
    attend = functools.partial(_attention, lam_init=lam_init)
    og = lax.cond(score_bound <= SCORE_BOUND,
                  functools.partial(attend, bounded=True), functools.partial(attend, bounded=False),
                  q, kt, v, sz, b_lambda[0], b_subln_g[0])
    out = _out_proj(xs, og, b_w_out[0].astype(BF16))
    return out.reshape(bsz, s, d)
```

```python
import functools
import math

import jax
import jax.numpy as jnp
from jax import lax
from jax.experimental import pallas as pl
from jax.experimental.pallas import tpu as pltpu

EPS = 1e-6
CHUNK = 64
CONV_WIDTH = 31
HEAD_DIM = 128
V_DIM = 2 * HEAD_DIM
LANES = 128
SUBLANES = 8
HALO_ROWS = 32
VMEM_LIMIT = 60 * 1024 * 1024
LOG2E = 1.4426950408889634
NEG_BIG = -1e30
SCORE_BOUND = 64.0
BF16_MARGIN = 1.02

BF16 = jnp.bfloat16
F32 = jnp.float32


def _sigmoid(v):
    return 1.0 / (1.0 + jnp.exp(-v))


def _dot(a, b):
    return jnp.dot(a, b, preferred_element_type=F32)


def _lane_tile(a, n):
    return jnp.concatenate([a] * n, axis=1)


def _params(*sem):
    return pltpu.CompilerParams(dimension_semantics=sem, vmem_limit_bytes=VMEM_LIMIT)


def _conf_in_kernel(x_ref, g_ref, wa_ref, wb_ref, wz_ref, y_ref, sz_ref, h_ref):
    @pl.when(pl.program_id(1) == 0)
    def _():
        x = x_ref[...]
        ms = jnp.mean(x * x, axis=-1, keepdims=True)
        h_ref[...] = (x * lax.rsqrt(ms + EPS) * g_ref[...]).astype(BF16)

    h = h_ref[...]
    gate = _sigmoid(_dot(h, wb_ref[...]))
    z = _dot(h, wz_ref[...])
    sz_ref[...] = (z * _sigmoid(z)).astype(BF16)
    y_ref[...] = (_dot(h, wa_ref[...]) * gate).astype(BF16)


def _conf_in(x, norm_g, w_in, *, tm=512, tn=1024):
    s, d = x.shape
    e = w_in.shape[1] // 3
    nj = e // tn
    return pl.pallas_call(
        _conf_in_kernel,
        grid=(s // tm, nj),
        in_specs=[
            pl.BlockSpec((tm, d), lambda i, j: (i, 0)),
            pl.BlockSpec((1, d), lambda i, j: (0, 0)),
            pl.BlockSpec((d, tn), lambda i, j: (0, j)),
            pl.BlockSpec((d, tn), lambda i, j: (0, j + nj)),
            pl.BlockSpec((d, tn), lambda i, j: (0, j + 2 * nj)),
        ],
        out_specs=[
            pl.BlockSpec((tm, tn), lambda i, j: (i, j)),
            pl.BlockSpec((tm, tn), lambda i, j: (i, j)),
        ],
        out_shape=[jax.ShapeDtypeStruct((s, e), BF16), jax.ShapeDtypeStruct((s, e), BF16)],
        scratch_shapes=[pltpu.VMEM((tm, d), BF16)],
        compiler_params=_params("arbitrary", "arbitrary"),
        name="conf_in",
    )(x, norm_g.reshape(1, d), w_in, w_in, w_in)


def _conf_tail_kernel(x_ref, ym_ref, yh_ref, sz_ref, dww_ref, dwb_ref, lng_ref, lnb_ref, wout_ref,
                      o_ref, ybuf_ref, cbuf_ref, *, tm, rows):
    e = ym_ref.shape[1]
    halo = yh_ref[...].astype(F32)
    ybuf_ref[0:HALO_ROWS, :] = jnp.where(pl.program_id(0) > 0, halo, 0.0)
    ybuf_ref[HALO_ROWS:, :] = ym_ref[...].astype(F32)
    first = HALO_ROWS - (CONV_WIDTH - 1)

    def lane_body(g, carry):
        lanes = pl.ds(pl.multiple_of(g * LANES, LANES), LANES)
        for rc in range(tm // rows):
            acc = jnp.broadcast_to(dwb_ref[:, lanes], (rows, LANES))
            for r in range(SUBLANES):
                span = rows + SUBLANES if r else rows
                z = None
                for k in range(CONV_WIDTH):
                    if (first + k) % SUBLANES != r:
                        continue
                    aligned = rc * rows + first + k - r
                    term = ybuf_ref[pl.ds(aligned, span), lanes] * dww_ref[pl.ds(k, 1), lanes]
                    z = term if z is None else z + term
                acc = acc + z[r:r + rows]
            cbuf_ref[pl.ds(rc * rows, rows), lanes] = acc
        return carry

    lax.fori_loop(0, e // LANES, lane_body, 0)

    c = cbuf_ref[...]
    mu = jnp.mean(c, axis=-1, keepdims=True)
    xc = c - mu
    var = jnp.mean(xc * xc, axis=-1, keepdims=True)
    yn = xc * lax.rsqrt(var + EPS) * lng_ref[...] + lnb_ref[...]
    act = yn * _sigmoid(yn) * sz_ref[...].astype(F32)
    o_ref[...] = x_ref[...] + _dot(act.astype(BF16), wout_ref[...])


def _conf_tail(x, y, sz, dw_w, dw_b, ln_g, ln_b, w_out, *, tm=256, rows=256):
    s, d = x.shape
    e = y.shape[1]
    hb = tm // HALO_ROWS
    row = lambda i: (i, 0)
    const = lambda i: (0, 0)
    return pl.pallas_call(
        functools.partial(_conf_tail_kernel, tm=tm, rows=rows),
        grid=(s // tm,),
        in_specs=[
            pl.BlockSpec((tm, d), row),
            pl.BlockSpec((tm, e), row),
            pl.BlockSpec((HALO_ROWS, e), lambda i: (jnp.maximum(i * hb - 1, 0), 0)),
            pl.BlockSpec((tm, e), row),
            pl.BlockSpec((CONV_WIDTH, e), const),
            pl.BlockSpec((1, e), const),
            pl.BlockSpec((1, e), const),
            pl.BlockSpec((1, e), const),
            pl.BlockSpec((e, d), const),
        ],
        out_specs=pl.BlockSpec((tm, d), row),
        out_shape=jax.ShapeDtypeStruct((s, d), F32),
        scratch_shapes=[pltpu.VMEM((tm + HALO_ROWS, e), F32), pltpu.VMEM((tm, e), F32)],
        compiler_params=_params("arbitrary"),
        name="conf_tail",
    )(x, y, y, sz, dw_w, dw_b.reshape(1, e), ln_g.reshape(1, e), ln_b.reshape(1, e), w_out)


def _group_rms(u, g_row, scale):
    outs = []
    for c in range(u.shape[1] // HEAD_DIM):
        blk = u[:, c * HEAD_DIM:(c + 1) * HEAD_DIM]
        ms = jnp.mean(blk * blk, axis=-1, keepdims=True)
        outs.append(blk * lax.rsqrt(ms + EPS) * (g_row * scale))
    return jnp.concatenate(outs, axis=1)


def _qkv_kernel(x_ref, gkv_ref, gb_ref, kg_ref, qg_ref, wk_ref, wv_ref, wq_ref, wz_ref,
                kt_ref, v_ref, q_ref, sz_ref, hkv_ref, hb_ref, *, q_scale):
    @pl.when(pl.program_id(1) == 0)
    def _():
        x = x_ref[...]
        r = x * lax.rsqrt(jnp.mean(x * x, axis=-1, keepdims=True) + EPS)
        hkv_ref[...] = (r * gkv_ref[...]).astype(BF16)
        hb_ref[...] = (r * gb_ref[...]).astype(BF16)

    hkv = hkv_ref[...]
    hb = hb_ref[...]
    kt_ref[...] = _group_rms(_dot(hkv, wk_ref[...]), kg_ref[...], 1.0).T.astype(BF16)
    q_ref[...] = _group_rms(_dot(hb, wq_ref[...]), qg_ref[...], q_scale).astype(BF16)
    z = _dot(hb, wz_ref[...])
    sz_ref[...] = (z * _sigmoid(z)).astype(BF16)
    v_ref[...] = _dot(hkv, wv_ref[...]).astype(BF16)


def _qkv(x, kv_norm_g, b_norm_g, k_norm_g, q_norm_g, w_kv, w_b, *, q_scale, tm=512, tn=512):
    s, d = x.shape
    n = w_kv.shape[1] // 2
    nj = n // tn
    vec = lambda c: pl.BlockSpec((1, c), lambda i, j: (0, 0))
    w_lo = pl.BlockSpec((d, tn), lambda i, j: (0, j))
    w_hi = pl.BlockSpec((d, tn), lambda i, j: (0, j + nj))
    out = pl.BlockSpec((tm, tn), lambda i, j: (i, j))
    act = jax.ShapeDtypeStruct((s, n), BF16)
    return pl.pallas_call(
        functools.partial(_qkv_kernel, q_scale=q_scale),
        grid=(s // tm, nj),
        in_specs=[pl.BlockSpec((tm, d), lambda i, j: (i, 0)), vec(d), vec(d), vec(HEAD_DIM), vec(HEAD_DIM),
                  w_lo, w_hi, w_lo, w_hi],
        out_specs=[pl.BlockSpec((tn, tm), lambda i, j: (j, i)), out, out, out],
        out_shape=[jax.ShapeDtypeStruct((n, s), BF16), act, act, act],
        scratch_shapes=[pltpu.VMEM((tm, d), BF16), pltpu.VMEM((tm, d), BF16)],
        compiler_params=_params("arbitrary", "arbitrary"),
        name="qkv_proj",
    )(x, kv_norm_g.reshape(1, d), b_norm_g.reshape(1, d), k_norm_g.reshape(1, HEAD_DIM),
      q_norm_g.reshape(1, HEAD_DIM), w_kv, w_kv, w_b, w_b)


def _visible(rows, cols):
    shift = CHUNK.bit_length() - 1
    q_chunk = lax.broadcasted_iota(jnp.int32, (rows, cols), 0) >> shift
    k_chunk = lax.broadcasted_iota(jnp.int32, (rows, cols), 1) >> shift
    return k_chunk <= q_chunk


def _attn_finish(lam_ref, sz_ref, subg_ref, o_ref, softmax_out, lam_init, chunk):
    lp = lam_ref[...]
    lam = (jnp.exp(jnp.sum(lp[0:1] * lp[1:2], axis=-1, keepdims=True))
           - jnp.exp(jnp.sum(lp[2:3] * lp[3:4], axis=-1, keepdims=True)) + lam_init)
    for r in range(0, o_ref.shape[0], chunk):
        rows = slice(r, r + chunk)
        o = softmax_out(0, rows) - lam * softmax_out(1, rows)
        ms = jnp.mean(o * o, axis=-1, keepdims=True)
        o = o * lax.rsqrt(ms + EPS) * subg_ref[...] * (1.0 - lam_init)
        o_ref[rows, :] = (o * sz_ref[rows, :].astype(F32)).astype(BF16)


def _map_cols(m):
    return slice(m * HEAD_DIM, (m + 1) * HEAD_DIM)


def _attn_online_kernel(lam_ref, q_ref, kt_ref, v_ref, sz_ref, subg_ref, o_ref, acc_ref, m_ref, l_ref,
                        *, t, lam_init):
    qi = pl.program_id(1)
    acc_ref[...] = jnp.zeros_like(acc_ref)
    l_ref[...] = jnp.zeros_like(l_ref)
    m_ref[...] = jnp.full_like(m_ref, NEG_BIG)

    def step(j, masked):
        cols = pl.ds(pl.multiple_of(j * t, t), t)
        kts = kt_ref[:, cols]
        vs = v_ref[cols, :]
        for m in range(2):
            sc = _dot(q_ref[:, _map_cols(m)], kts[_map_cols(m), :])
            if masked:
                sc = jnp.where(_visible(t, t), sc, NEG_BIG)
            m_prev = m_ref[m]
            m_next = jnp.maximum(m_prev, jnp.max(sc, axis=1, keepdims=True))
            p = jnp.exp2(sc - _lane_tile(m_next, t // LANES))
            alpha = jnp.exp2(m_prev - m_next)
            l_ref[m] = alpha * l_ref[m] + jnp.sum(p, axis=1, keepdims=True)
            m_ref[m] = m_next
            acc_ref[m] = acc_ref[m] * _lane_tile(alpha, V_DIM // LANES) + _dot(p.astype(BF16), vs)

    def body(j, carry):
        step(j, False)
        return carry

    lax.fori_loop(0, qi, body, 0)
    step(qi, True)

    _attn_finish(lam_ref, sz_ref, subg_ref, o_ref,
                 lambda m, rows: acc_ref[m, rows, :] * (1.0 / l_ref[m, rows, 0:1]), lam_init, t)


def _attn_bounded_kernel(lam_ref, q_ref, kt_ref, v_ref, sz_ref, subg_ref, o_ref, acc_ref, l_ref,
                         sa_ref, sb_ref, *, tq, tk, lam_init):
    qi = pl.program_id(1)
    ratio = tq // tk
    assert ratio % 2 == 0, "the pair loop covers the qi * ratio fully visible kv blocks"
    acc_ref[...] = jnp.zeros_like(acc_ref)
    l_ref[...] = jnp.zeros_like(l_ref)

    def scores(j, s_ref, row0=0):
        kts = kt_ref[:, pl.ds(pl.multiple_of(j * tk, tk), tk)]
        for r in range(row0, tq, tk):
            for m in range(2):
                s_ref[m, r:r + tk, :] = _dot(q_ref[r:r + tk, _map_cols(m)], kts[_map_cols(m), :])

    def consume(j, s_ref, diag_row0=None):
        row0 = 0 if diag_row0 is None else diag_row0
        vs = v_ref[pl.ds(pl.multiple_of(j * tk, tk), tk), :]
        for r in range(row0, tq, tk):
            for m in range(2):
                sc = s_ref[m, r:r + tk, :]
                if r == diag_row0:
                    sc = jnp.where(_visible(tk, tk), sc, NEG_BIG)
                p = jnp.exp2(sc)
                part = p[:, 0:LANES]
                for c in range(1, tk // LANES):
                    part = part + p[:, c * LANES:(c + 1) * LANES]
                l_ref[m, r:r + tk, :] += part
                acc_ref[m, r:r + tk, :] += _dot(p.astype(BF16), vs)

    n_full = qi * ratio
    scores(0, sa_ref)

    def pair(j):
        scores(j + 1, sb_ref)
        consume(j, sa_ref)
        scores(j + 2, sa_ref)
        consume(j + 1, sb_ref)

    def two_pairs(jj, carry):
        pair(4 * jj)
        pair(4 * jj + 2)
        return carry

    n_pairs = n_full // 2
    lax.fori_loop(0, n_pairs // 2, two_pairs, 0)

    @pl.when(n_pairs % 2 == 1)
    def _():
        pair(n_full - 2)

    bufs = (sa_ref, sb_ref)
    for d in range(ratio):
        if d + 1 < ratio:
            scores(n_full + d + 1, bufs[(d + 1) % 2], row0=(d + 1) * tk)
        consume(n_full + d, bufs[d % 2], diag_row0=d * tk)

    _attn_finish(lam_ref, sz_ref, subg_ref, o_ref,
                 lambda m, rows: acc_ref[m, rows, :] * (1.0 / jnp.sum(l_ref[m, rows, :], axis=1, keepdims=True)),
                 lam_init, tk)


def _attention(q, kt, v, sz, lam_p, subln_g, *, lam_init, bounded, tq=1024, tk=512):
    s, n = q.shape
    heads = n // V_DIM
    if bounded:
        body = functools.partial(_attn_bounded_kernel, tq=tq, tk=tk, lam_init=lam_init)
        scores = pltpu.VMEM((2, tq, tk), F32)
        scratch = [pltpu.VMEM((2, tq, V_DIM), F32), pltpu.VMEM((2, tq, LANES), F32), scores, scores]
    else:
        tq = tk
        body = functools.partial(_attn_online_kernel, t=tk, lam_init=lam_init)
        stats = pltpu.VMEM((2, tq, LANES), F32)
        scratch = [pltpu.VMEM((2, tq, V_DIM), F32), stats, stats]
    blk = pl.BlockSpec((tq, V_DIM), lambda h, i: (i, h))
    return pl.pallas_call(
        body,
        grid=(heads, s // tq),
        in_specs=[pl.BlockSpec(lam_p.shape, lambda h, i: (0, 0)), blk,
                  pl.BlockSpec((V_DIM, s), lambda h, i: (h, 0)), pl.BlockSpec((s, V_DIM), lambda h, i: (0, h)), blk,
                  pl.BlockSpec((1, V_DIM), lambda h, i: (0, 0))],
        out_specs=blk,
        out_shape=jax.ShapeDtypeStruct((s, n), BF16),
        scratch_shapes=scratch,
        compiler_params=_params("arbitrary", "arbitrary"),
        name="diff_attn_bounded" if bounded else "diff_attn_online",
    )(lam_p, q, kt, v, sz, subln_g.reshape(1, V_DIM))


def _out_proj_kernel(x_ref, a_ref, w_ref, o_ref):
    o_ref[...] = x_ref[...] + _dot(a_ref[...], w_ref[...])


def _out_proj(x, a, w, *, tm=512):
    s, d = x.shape
    n = a.shape[1]
    return pl.pallas_call(
        _out_proj_kernel,
        grid=(s // tm,),
        in_specs=[pl.BlockSpec((tm, d), lambda i: (i, 0)), pl.BlockSpec((tm, n), lambda i: (i, 0)),
                  pl.BlockSpec((n, d), lambda i: (0, 0))],
        out_specs=pl.BlockSpec((tm, d), lambda i: (i, 0)),
        out_shape=jax.ShapeDtypeStruct((s, d), F32),
        compiler_params=_params("arbitrary"),
        name="out_proj",
    )(x, a, w)


def kernel(x, a_norm_g, a_w_in, a_dw_w, a_dw_b, a_ln_g, a_ln_b, a_w_out, kv_norm_g, w_kv, k_norm_g,
           b_norm_g, b_w_in, b_q_norm_g, b_lambda, b_subln_g, b_w_out):
    bsz, s, d = x.shape
    n_a, n_b = a_norm_g.shape[0], b_norm_g.shape[0]
    assert bsz == 1 and n_a == 1 and n_b == 1, "one conformer layer followed by one attention layer"
    xs = x.reshape(s, d)

    y, sz = _conf_in(xs, a_norm_g[0], a_w_in[0].astype(BF16))
    xs = _conf_tail(xs, y, sz, a_dw_w[0], a_dw_b[0], a_ln_g[0], a_ln_b[0], a_w_out[0].astype(BF16))

    layer_idx = n_a + 1
    lam_init = 0.8 - 0.6 * math.exp(-0.3 * (layer_idx - 1))
    q_scale = HEAD_DIM ** -0.5 * LOG2E
    kt, v, q, sz = _qkv(xs, kv_norm_g, b_norm_g[0], k_norm_g, b_q_norm_g[0], w_kv.astype(BF16),
                        b_w_in[0].astype(BF16), q_scale=q_scale)
    score_bound = (HEAD_DIM ** 0.5 * LOG2E * BF16_MARGIN
                   * jnp.max(jnp.abs(b_q_norm_g[0])) * jnp.max(jnp.abs(k_norm_g)))
    attend = functools.partial(_attention, lam_init=lam_init)
    og = lax.cond(score_bound <= SCORE_BOUND,
                  functools.partial(attend, bounded=True), functools.partial(attend, bounded=False),
                  q, kt, v, sz, b_lambda[0], b_subln_g[0])
    out = _out_proj(xs, og, b_w_out[0].astype(BF16))
    return out.reshape(bsz, s, d)
```

```python
import functools
import math

import jax
import jax.numpy as jnp
from jax import lax
from jax.experimental import pallas as pl
from jax.experimental.pallas import tpu as pltpu

EPS = 1e-6
CHUNK = 64
CONV_WIDTH = 31
HEAD_DIM = 128
V_DIM = 2 * HEAD_DIM
LANES = 128
SUBLANES = 8
HALO_ROWS = 32
VMEM_LIMIT = 60 * 1024 * 1024
LOG2E = 1.4426950408889634
NEG_BIG = -1e30
SCORE_BOUND = 64.0
BF16_MARGIN = 1.02

BF16 = jnp.bfloat16
F32 = jnp.float32


def _sigmoid(v):
    return 1.0 / (1.0 + jnp.exp(-v))


def _dot(a, b):
    return jnp.dot(a, b, preferred_element_type=F32)


def _lane_tile(a, n):
    return jnp.concatenate([a] * n, axis=1)


def _params(*sem):
    return pltpu.CompilerParams(dimension_semantics=sem, vmem_limit_bytes=VMEM_LIMIT)


def _conf_in_kernel(x_ref, g_ref, wa_ref, wb_ref, wz_ref, y_ref, sz_ref, h_ref):
    @pl.when(pl.program_id(1) == 0)
    def _():
        x = x_ref[...]
        ms = jnp.mean(x * x, axis=-1, keepdims=True)
        h_ref[...] = (x * lax.rsqrt(ms + EPS) * g_ref[...]).astype(BF16)

    h = h_ref[...]
    gate = _sigmoid(_dot(h, wb_ref[...]))
    z = _dot(h, wz_ref[...])
    sz_ref[...] = (z * _sigmoid(z)).astype(BF16)
    y_ref[...] = (_dot(h, wa_ref[...]) * gate).astype(BF16)


def _conf_in(x, norm_g, w_in, *, tm=512, tn=1024):
    s, d = x.shape
    e = w_in.shape[1] // 3
    nj = e // tn
    return pl.pallas_call(
        _conf_in_kernel,
        grid=(s // tm, nj),
        in_specs=[
            pl.BlockSpec((tm, d), lambda i, j: (i, 0)),
            pl.BlockSpec((1, d), lambda i, j: (0, 0)),
            pl.BlockSpec((d, tn), lambda i, j: (0, j)),
            pl.BlockSpec((d, tn), lambda i, j: (0, j + nj)),
            pl.BlockSpec((d, tn), lambda i, j: (0, j + 2 * nj)),
        ],
        out_specs=[
            pl.BlockSpec((tm, tn), lambda i, j: (i, j)),
            pl.BlockSpec((tm, tn), lambda i, j: (i, j)),
        ],
        out_shape=[jax.ShapeDtypeStruct((s, e), BF16), jax.ShapeDtypeStruct((s, e), BF16)],
        scratch_shapes=[pltpu.VMEM((tm, d), BF16)],
        compiler_params=_params("arbitrary", "arbitrary"),
        name="conf_in",
    )(x, norm_g.reshape(1, d), w_in, w_in, w_in)


def _conf_tail_kernel(x_ref, ym_ref, yh_ref, sz_ref, dww_ref, dwb_ref, lng_ref, lnb_ref, wout_ref,
                      o_ref, ybuf_ref, cbuf_ref, *, tm, rows):
    e = ym_ref.shape[1]
    halo = yh_ref[...].astype(F32)
    ybuf_ref[0:HALO_ROWS, :] = jnp.where(pl.program_id(0) > 0, halo, 0.0)
    ybuf_ref[HALO_ROWS:, :] = ym_ref[...].astype(F32)
    first = HALO_ROWS - (CONV_WIDTH - 1)

    def lane_body(g, carry):
        lanes = pl.ds(pl.multiple_of(g * LANES, LANES), LANES)
        for rc in range(tm // rows):
            acc = jnp.broadcast_to(dwb_ref[:, lanes], (rows, LANES))
            for r in range(SUBLANES):
                span = rows + SUBLANES if r else rows
                z = None
                for k in range(CONV_WIDTH):
                    if (first + k) % SUBLANES != r:
                        continue
                    aligned = rc * rows + first + k - r
                    term = ybuf_ref[pl.ds(aligned, span), lanes] * dww_ref[pl.ds(k, 1), lanes]
                    z = term if z is None else z + term
                acc = acc + z[r:r + rows]
            cbuf_ref[pl.ds(rc * rows, rows), lanes] = acc
        return carry

    lax.fori_loop(0, e // LANES, lane_body, 0)

    c = cbuf_ref[...]
    mu = jnp.mean(c, axis=-1, keepdims=True)
    xc = c - mu
    var = jnp.mean(xc * xc, axis=-1, keepdims=True)
    yn = xc * lax.rsqrt(var + EPS) * lng_ref[...] + lnb_ref[...]
    act = yn * _sigmoid(yn) * sz_ref[...].astype(F32)
    o_ref[...] = x_ref[...] + _dot(act.astype(BF16), wout_ref[...])


def _conf_tail(x, y, sz, dw_w, dw_b, ln_g, ln_b, w_out, *, tm=256, rows=256):
    s, d = x.shape
    e = y.shape[1]
    hb = tm // HALO_ROWS
    row = lambda i: (i, 0)
    const = lambda i: (0, 0)
    return pl.pallas_call(
        functools.partial(_conf_tail_kernel, tm=tm, rows=rows),
        grid=(s // tm,),
        in_specs=[
            pl.BlockSpec((tm, d), row),
            pl.BlockSpec((tm, e), row),
            pl.BlockSpec((HALO_ROWS, e), lambda i: (jnp.maximum(i * hb - 1, 0), 0)),
            pl.BlockSpec((tm, e), row),
            pl.BlockSpec((CONV_WIDTH, e), const),
            pl.BlockSpec((1, e), const),
            pl.BlockSpec((1, e), const),
            pl.BlockSpec((1, e), const),
            pl.BlockSpec((e, d), const),
        ],
        out_specs=pl.BlockSpec((tm, d), row),
        out_shape=jax.ShapeDtypeStruct((s, d), F32),
        scratch_shapes=[pltpu.VMEM((tm + HALO_ROWS, e), F32), pltpu.VMEM((tm, e), F32)],
        compiler_params=_params("arbitrary"),
        name="conf_tail",
    )(x, y, y, sz, dw_w, dw_b.reshape(1, e), ln_g.reshape(1, e), ln_b.reshape(1, e), w_out)


def _group_rms(u, g_row, scale):
    outs = []
    for c in range(u.shape[1] // HEAD_DIM):
        blk = u[:, c * HEAD_DIM:(c + 1) * HEAD_DIM]
        ms = jnp.mean(blk * blk, axis=-1, keepdims=True)
        outs.append(blk * lax.rsqrt(ms + EPS) * (g_row * scale))
    return jnp.concatenate(outs, axis=1)


def _qkv_kernel(x_ref, gkv_ref, gb_ref, kg_ref, qg_ref, wk_ref, wv_ref, wq_ref, wz_ref,
                kt_ref, v_ref, q_ref, sz_ref, hkv_ref, hb_ref, *, q_scale):
    @pl.when(pl.program_id(1) == 0)
    def _():
        x = x_ref[...]
        r = x * lax.rsqrt(jnp.mean(x * x, axis=-1, keepdims=True) + EPS)
        hkv_ref[...] = (r * gkv_ref[...]).astype(BF16)
        hb_ref[...] = (r * gb_ref[...]).astype(BF16)

    hkv = hkv_ref[...]
    hb = hb_ref[...]
    kt_ref[...] = _group_rms(_dot(hkv, wk_ref[...]), kg_ref[...], 1.0).T.astype(BF16)
    q_ref[...] = _group_rms(_dot(hb, wq_ref[...]), qg_ref[...], q_scale).astype(BF16)
    z = _dot(hb, wz_ref[...])
    sz_ref[...] = (z * _sigmoid(z)).astype(BF16)
    v_ref[...] = _dot(hkv, wv_ref[...]).astype(BF16)


def _qkv(x, kv_norm_g, b_norm_g, k_norm_g, q_norm_g, w_kv, w_b, *, q_scale, tm=512, tn=512):
    s, d = x.shape
    n = w_kv.shape[1] // 2
    nj = n // tn
    vec = lambda c: pl.BlockSpec((1, c), lambda i, j: (0, 0))
    w_lo = pl.BlockSpec((d, tn), lambda i, j: (0, j))
    w_hi = pl.BlockSpec((d, tn), lambda i, j: (0, j + nj))
    out = pl.BlockSpec((tm, tn), lambda i, j: (i, j))
    act = jax.ShapeDtypeStruct((s, n), BF16)
    return pl.pallas_call(
        functools.partial(_qkv_kernel, q_scale=q_scale),
        grid=(s // tm, nj),
        in_specs=[pl.BlockSpec((tm, d), lambda i, j: (i, 0)), vec(d), vec(d), vec(HEAD_DIM), vec(HEAD_DIM),
                  w_lo, w_hi, w_lo, w_hi],
        out_specs=[pl.BlockSpec((tn, tm), lambda i, j: (j, i)), out, out, out],
        out_shape=[jax.ShapeDtypeStruct((n, s), BF16), act, act, act],
        scratch_shapes=[pltpu.VMEM((tm, d), BF16), pltpu.VMEM((tm, d), BF16)],
        compiler_params=_params("arbitrary", "arbitrary"),
        name="qkv_proj",
    )(x, kv_norm_g.reshape(1, d), b_norm_g.reshape(1, d), k_norm_g.reshape(1, HEAD_DIM),
      q_norm_g.reshape(1, HEAD_DIM), w_kv, w_kv, w_b, w_b)


def _visible(rows, cols):
    shift = CHUNK.bit_length() - 1
    q_chunk = lax.broadcasted_iota(jnp.int32, (rows, cols), 0) >> shift
    k_chunk = lax.broadcasted_iota(jnp.int32, (rows, cols), 1) >> shift
    return k_chunk <= q_chunk


def _attn_finish(lam_ref, sz_ref, subg_ref, o_ref, softmax_out, lam_init, chunk):
    lp = lam_ref[...]
    lam = (jnp.exp(jnp.sum(lp[0:1] * lp[1:2], axis=-1, keepdims=True))
           - jnp.exp(jnp.sum(lp[2:3] * lp[3:4], axis=-1, keepdims=True)) + lam_init)
    for r in range(0, o_ref.shape[0], chunk):
        rows = slice(r, r + chunk)
        o = softmax_out(0, rows) - lam * softmax_out(1, rows)
        ms = jnp.mean(o * o, axis=-1, keepdims=True)
        o = o * lax.rsqrt(ms + EPS) * subg_ref[...] * (1.0 - lam_init)
        o_ref[rows, :] = (o * sz_ref[rows, :].astype(F32)).astype(BF16)


def _map_cols(m):
    return slice(m * HEAD_DIM, (m + 1) * HEAD_DIM)


def _attn_online_kernel(lam_ref, q_ref, kt_ref, v_ref, sz_ref, subg_ref, o_ref, acc_ref, m_ref, l_ref,
                        *, t, lam_init):
    qi = pl.program_id(1)
    acc_ref[...] = jnp.zeros_like(acc_ref)
    l_ref[...] = jnp.zeros_like(l_ref)
    m_ref[...] = jnp.full_like(m_ref, NEG_BIG)

    def step(j, masked):
        cols = pl.ds(pl.multiple_of(j * t, t), t)
        kts = kt_ref[:, cols]
        vs = v_ref[cols, :]
        for m in range(2):
            sc = _dot(q_ref[:, _map_cols(m)], kts[_map_cols(m), :])
            if masked:
                sc = jnp.where(_visible(t, t), sc, NEG_BIG)
            m_prev = m_ref[m]
            m_next = jnp.maximum(m_prev, jnp.max(sc, axis=1, keepdims=True))
            p = jnp.exp2(sc - _lane_tile(m_next, t // LANES))
            alpha = jnp.exp2(m_prev - m_next)
            l_ref[m] = alpha * l_ref[m] + jnp.sum(p, axis=1, keepdims=True)
            m_ref[m] = m_next
            acc_ref[m] = acc_ref[m] * _lane_tile(alpha, V_DIM // LANES) + _dot(p.astype(BF16), vs)

    def body(j, carry):
        step(j, False)
        return carry

    lax.fori_loop(0, qi, body, 0)
    step(qi, True)

    _attn_finish(lam_ref, sz_ref, subg_ref, o_ref,
                 lambda m, rows: acc_ref[m, rows, :] * (1.0 / l_ref[m, rows, 0:1]), lam_init, t)


def _attn_bounded_kernel(lam_ref, q_ref, qn_ref, kt_ref, v_ref, sz_ref, subg_ref, o_ref, acc_ref, l_ref,
                         sa_ref, sb_ref, *, tq, tk, lam_init):
    qi = pl.program_id(1)
    last = pl.num_programs(1) - 1
    ratio = tq // tk
    assert ratio % 2 == 0, "the pair loop covers the qi * ratio fully visible kv blocks"
    acc_ref[...] = jnp.zeros_like(acc_ref)
    l_ref[...] = jnp.zeros_like(l_ref)

    def scores(j, s_ref, row0=0, queries=q_ref):
        kts = kt_ref[:, pl.ds(pl.multiple_of(j * tk, tk), tk)]
        for r in range(row0, tq, tk):
            for m in range(2):
                s_ref[m, r:r + tk, :] = _dot(queries[r:r + tk, _map_cols(m)], kts[_map_cols(m), :])

    def consume(j, s_ref, diag_row0=None):
        row0 = 0 if diag_row0 is None else diag_row0
        vs = v_ref[pl.ds(pl.multiple_of(j * tk, tk), tk), :]
        for r in range(row0, tq, tk):
            for m in range(2):
                sc = s_ref[m, r:r + tk, :]
                if r == diag_row0:
                    sc = jnp.where(_visible(tk, tk), sc, NEG_BIG)
                p = jnp.exp2(sc)
                part = p[:, 0:LANES]
                for c in range(1, tk // LANES):
                    part = part + p[:, c * LANES:(c + 1) * LANES]
                l_ref[m, r:r + tk, :] += part
                acc_ref[m, r:r + tk, :] += _dot(p.astype(BF16), vs)

    n_full = qi * ratio

    @pl.when(qi == 0)
    def _():
        scores(0, sa_ref)

    def pair(j):
        scores(j + 1, sb_ref)
        consume(j, sa_ref)
        scores(j + 2, sa_ref)
        consume(j + 1, sb_ref)

    def two_pairs(jj, carry):
        pair(4 * jj)
        pair(4 * jj + 2)
        return carry

    n_pairs = n_full // 2
    lax.fori_loop(0, n_pairs // 2, two_pairs, 0)

    @pl.when(n_pairs % 2 == 1)
    def _():
        pair(n_full - 2)

    bufs = (sa_ref, sb_ref)
    for d in range(ratio):
        if d + 1 < ratio:
            scores(n_full + d + 1, bufs[(d + 1) % 2], row0=(d + 1) * tk)
        consume(n_full + d, bufs[d % 2], diag_row0=d * tk)

    def finish():
        _attn_finish(lam_ref, sz_ref, subg_ref, o_ref,
                     lambda m, rows: acc_ref[m, rows, :] * (1.0 / jnp.sum(l_ref[m, rows, :], axis=1, keepdims=True)),
                     lam_init, tk)

    @pl.when(qi < last)
    def _():
        scores(0, sa_ref, queries=qn_ref)
        finish()

    @pl.when(qi == last)
    def _():
        finish()


def _attention(q, kt, v, sz, lam_p, subln_g, *, lam_init, bounded, tq=1024, tk=512):
    s, n = q.shape
    heads = n // V_DIM
    if bounded:
        body = functools.partial(_attn_bounded_kernel, tq=tq, tk=tk, lam_init=lam_init)
        scores = pltpu.VMEM((2, tq, tk), F32)
        scratch = [pltpu.VMEM((2, tq, V_DIM), F32), pltpu.VMEM((2, tq, LANES), F32), scores, scores]
    else:
        tq = tk
        body = functools.partial(_attn_online_kernel, t=tk, lam_init=lam_init)
        stats = pltpu.VMEM((2, tq, LANES), F32)
        scratch = [pltpu.VMEM((2, tq, V_DIM), F32), stats, stats]
    n_tiles = s // tq
    blk = pl.BlockSpec((tq, V_DIM), lambda h, i: (i, h))
    queries, query_specs = [q], [blk]
    if bounded:
        queries.append(q)
        query_specs.append(pl.BlockSpec((tq, V_DIM), lambda h, i: (jnp.minimum(i + 1, n_tiles - 1), h)))
    return pl.pallas_call(
        body,
        grid=(heads, n_tiles),
        in_specs=[pl.BlockSpec(lam_p.shape, lambda h, i: (0, 0)), *query_specs,
                  pl.BlockSpec((V_DIM, s), lambda h, i: (h, 0)), pl.BlockSpec((s, V_DIM), lambda h, i: (0, h)), blk,
                  pl.BlockSpec((1, V_DIM), lambda h, i: (0, 0))],
        out_specs=blk,
        out_shape=jax.ShapeDtypeStruct((s, n), BF16),
        scratch_shapes=scratch,
        compiler_params=_params("arbitrary", "arbitrary"),
        name="diff_attn_bounded" if bounded else "diff_attn_online",
    )(lam_p, *queries, kt, v, sz, subln_g.reshape(1, V_DIM))


def _out_proj_kernel(x_ref, a_ref, w_ref, o_ref):
    o_ref[...] = x_ref[...] + _dot(a_ref[...], w_ref[...])


def _out_proj(x, a, w, *, tm=512):
    s, d = x.shape
    n = a.shape[1]
    return pl.pallas_call(
        _out_proj_kernel,
        grid=(s // tm,),
        in_specs=[pl.BlockSpec((tm, d), lambda i: (i, 0)), pl.BlockSpec((tm, n), lambda i: (i, 0)),
                  pl.BlockSpec((n, d), lambda i: (0, 0))],
        out_specs=pl.BlockSpec((tm, d), lambda i: (i, 0)),
        out_shape=jax.ShapeDtypeStruct((s, d), F32),
        compiler_params=_params("arbitrary"),
        name="out_proj",
    )(x, a, w)


def kernel(x, a_norm_g, a_w_in, a_dw_w, a_dw_b, a_ln_g, a_ln_b, a_w_out, kv_norm_g, w_kv, k_norm_g,
           b_norm_g, b_w_in, b_q_norm_g, b_lambda, b_subln_g, b_w_out):
    bsz, s, d = x.shape
    n_a, n_b = a_norm_g.shape[0], b_norm_g.shape[0]
    assert bsz == 1 and n_a == 1 and n_b == 1, "one conformer layer followed by one attention layer"
    xs = x.reshape(s, d)

    y, sz = _conf_in(xs, a_norm_g[0], a_w_in[0].astype(BF16))
    xs = _conf_tail(xs, y, sz, a_dw_w[0], a_dw_b[0], a_ln_g[0], a_ln_b[0], a_w_out[0].astype(BF16))

    layer_idx = n_a + 1
    lam_init = 0.8 - 0.6 * math.exp(-0.3 * (layer_idx - 1))
    q_scale = HEAD_DIM ** -0.5 * LOG2E
    kt, v, q, sz = _qkv(xs, kv_norm_g, b_norm_g[0], k_norm_g, b_q_norm_g[0], w_kv.astype(BF16),
                        b_w_in[0].astype(BF16), q_scale=q_scale)
    score_bound = (HEAD_DIM ** 0.5 * LOG2E * BF16_MARGIN
                   * jnp.max(jnp.abs(b_q_norm_g[0])) * jnp.max(jnp.abs(k_norm_g)))
    attend = functools.partial(_attention, lam_init=lam_init)
    og = lax.cond(score_bound <= SCORE_BOUND,
                  functools.partial(attend, bounded=True), functools.partial(attend, bounded=False),
                  q, kt, v, sz, b_lambda[0], b_subln_g[0])
    out = _out_proj(xs, og, b_w_out[0].astype(BF16))
    return out.reshape(bsz, s, d)
```

```python
import functools
import math

import jax
import jax.numpy as jnp
from jax import lax
from jax.experimental import pallas as pl
from jax.experimental.pallas import tpu as pltpu

EPS = 1e-6
CHUNK = 64
CONV_WIDTH = 31
HEAD_DIM = 128
V_DIM = 2 * HEAD_DIM
LANES = 128
SUBLANES = 8
HALO_ROWS = 32
VMEM_LIMIT = 60 * 1024 * 1024
LOG2E = 1.4426950408889634
NEG_BIG = -1e30
SCORE_BOUND = 64.0
BF16_MARGIN = 1.02

BF16 = jnp.bfloat16
F32 = jnp.float32


def _sigmoid(v):
    return 1.0 / (1.0 + jnp.exp(-v))


def _dot(a, b):
    return jnp.dot(a, b, preferred_element_type=F32)


def _lane_tile(a, n):
    return jnp.concatenate([a] * n, axis=1)


def _params(*sem):
    return pltpu.CompilerParams(dimension_semantics=sem, vmem_limit_bytes=VMEM_LIMIT)


def _conf_in_kernel(x_ref, g_ref, wa_ref, wb_ref, wz_ref, *rest, n_later):
    later_f32, (y_ref, sz_ref), later_bf16, h_ref = (
        rest[:n_later], rest[n_later:n_later + 2], rest[n_later + 2:2 * n_later + 2], rest[-1])

    @pl.when(pl.program_id(1) == 0)
    def _():
        x = x_ref[...]
        ms = jnp.mean(x * x, axis=-1, keepdims=True)
        h_ref[...] = (x * lax.rsqrt(ms + EPS) * g_ref[...]).astype(BF16)

    h = h_ref[...]
    gate = _sigmoid(_dot(h, wb_ref[...]))
    z = _dot(h, wz_ref[...])
    sz_ref[...] = (z * _sigmoid(z)).astype(BF16)
    y_ref[...] = (_dot(h, wa_ref[...]) * gate).astype(BF16)

    for src, dst in zip(later_f32, later_bf16):
        dst[...] = src[...].astype(BF16)


def _conf_in(x, norm_g, w_in, later_weights, *, tm=512, tn=1024):
    s, d = x.shape
    e = w_in.shape[1] // 3
    nj = e // tn
    steps = (s // tm) * nj
    slab = lambda w: pl.BlockSpec((w.shape[0] // steps, w.shape[1]), lambda i, j: (i * nj + j, 0))
    assert all(w.shape[0] % (steps * 16) == 0 for w in later_weights), "bf16 slabs need 16-row multiples"
    act = jax.ShapeDtypeStruct((s, e), BF16)
    outs = pl.pallas_call(
        functools.partial(_conf_in_kernel, n_later=len(later_weights)),
        grid=(s // tm, nj),
        in_specs=[
            pl.BlockSpec((tm, d), lambda i, j: (i, 0)),
            pl.BlockSpec((1, d), lambda i, j: (0, 0)),
            pl.BlockSpec((d, tn), lambda i, j: (0, j)),
            pl.BlockSpec((d, tn), lambda i, j: (0, j + nj)),
            pl.BlockSpec((d, tn), lambda i, j: (0, j + 2 * nj)),
            *[slab(w) for w in later_weights],
        ],
        out_specs=[
            pl.BlockSpec((tm, tn), lambda i, j: (i, j)),
            pl.BlockSpec((tm, tn), lambda i, j: (i, j)),
            *[slab(w) for w in later_weights],
        ],
        out_shape=[act, act, *[jax.ShapeDtypeStruct(w.shape, BF16) for w in later_weights]],
        scratch_shapes=[pltpu.VMEM((tm, d), BF16)],
        compiler_params=_params("arbitrary", "arbitrary"),
        name="conf_in",
    )(x, norm_g.reshape(1, d), w_in, w_in, w_in, *later_weights)
    return outs[0], outs[1], outs[2:]


def _conf_tail_kernel(x_ref, ym_ref, yh_ref, sz_ref, dww_ref, dwb_ref, lng_ref, lnb_ref, wout_ref,
                      o_ref, ybuf_ref, cbuf_ref, *, tm, rows):
    e = ym_ref.shape[1]
    halo = yh_ref[...].astype(F32)
    ybuf_ref[0:HALO_ROWS, :] = jnp.where(pl.program_id(0) > 0, halo, 0.0)
    ybuf_ref[HALO_ROWS:, :] = ym_ref[...].astype(F32)
    first = HALO_ROWS - (CONV_WIDTH - 1)

    def lane_body(g, carry):
        lanes = pl.ds(pl.multiple_of(g * LANES, LANES), LANES)
        for rc in range(tm // rows):
            acc = jnp.broadcast_to(dwb_ref[:, lanes], (rows, LANES))
            for r in range(SUBLANES):
                span = rows + SUBLANES if r else rows
                z = None
                for k in range(CONV_WIDTH):
                    if (first + k) % SUBLANES != r:
                        continue
                    aligned = rc * rows + first + k - r
                    term = ybuf_ref[pl.ds(aligned, span), lanes] * dww_ref[pl.ds(k, 1), lanes]
                    z = term if z is None else z + term
                acc = acc + z[r:r + rows]
            cbuf_ref[pl.ds(rc * rows, rows), lanes] = acc
        return carry

    lax.fori_loop(0, e // LANES, lane_body, 0)

    c = cbuf_ref[...]
    mu = jnp.mean(c, axis=-1, keepdims=True)
    xc = c - mu
    var = jnp.mean(xc * xc, axis=-1, keepdims=True)
    yn = xc * lax.rsqrt(var + EPS) * lng_ref[...] + lnb_ref[...]
    act = yn * _sigmoid(yn) * sz_ref[...].astype(F32)
    o_ref[...] = x_ref[...] + _dot(act.astype(BF16), wout_ref[...])


def _conf_tail(x, y, sz, dw_w, dw_b, ln_g, ln_b, w_out, *, tm=256, rows=256):
    s, d = x.shape
    e = y.shape[1]
    hb = tm // HALO_ROWS
    row = lambda i: (i, 0)
    const = lambda i: (0, 0)
    return pl.pallas_call(
        functools.partial(_conf_tail_kernel, tm=tm, rows=rows),
        grid=(s // tm,),
        in_specs=[
            pl.BlockSpec((tm, d), row),
            pl.BlockSpec((tm, e), row),
            pl.BlockSpec((HALO_ROWS, e), lambda i: (jnp.maximum(i * hb - 1, 0), 0)),
            pl.BlockSpec((tm, e), row),
            pl.BlockSpec((CONV_WIDTH, e), const),
            pl.BlockSpec((1, e), const),
            pl.BlockSpec((1, e), const),
            pl.BlockSpec((1, e), const),
            pl.BlockSpec((e, d), const),
        ],
        out_specs=pl.BlockSpec((tm, d), row),
        out_shape=jax.ShapeDtypeStruct((s, d), F32),
        scratch_shapes=[pltpu.VMEM((tm + HALO_ROWS, e), F32), pltpu.VMEM((tm, e), F32)],
        compiler_params=_params("arbitrary"),
        name="conf_tail",
    )(x, y, y, sz, dw_w, dw_b.reshape(1, e), ln_g.reshape(1, e), ln_b.reshape(1, e), w_out)


def _group_rms(u, g_row, scale):
    outs = []
    for c in range(u.shape[1] // HEAD_DIM):
        blk = u[:, c * HEAD_DIM:(c + 1) * HEAD_DIM]
        ms = jnp.mean(blk * blk, axis=-1, keepdims=True)
        outs.append(blk * lax.rsqrt(ms + EPS) * (g_row * scale))
    return jnp.concatenate(outs, axis=1)


def _qkv_kernel(x_ref, gkv_ref, gb_ref, kg_ref, qg_ref, wk_ref, wv_ref, wq_ref, wz_ref,
                kt_ref, v_ref, q_ref, sz_ref, hkv_ref, hb_ref, *, q_scale):
    @pl.when(pl.program_id(1) == 0)
    def _():
        x = x_ref[...]
        r = x * lax.rsqrt(jnp.mean(x * x, axis=-1, keepdims=True) + EPS)
        hkv_ref[...] = (r * gkv_ref[...]).astype(BF16)
        hb_ref[...] = (r * gb_ref[...]).astype(BF16)

    hkv = hkv_ref[...]
    hb = hb_ref[...]
    kt_ref[...] = _group_rms(_dot(hkv, wk_ref[...]), kg_ref[...], 1.0).T.astype(BF16)
    q_ref[...] = _group_rms(_dot(hb, wq_ref[...]), qg_ref[...], q_scale).astype(BF16)
    z = _dot(hb, wz_ref[...])
    sz_ref[...] = (z * _sigmoid(z)).astype(BF16)
    v_ref[...] = _dot(hkv, wv_ref[...]).astype(BF16)


def _qkv(x, kv_norm_g, b_norm_g, k_norm_g, q_norm_g, w_kv, w_b, *, q_scale, tm=512, tn=512):
    s, d = x.shape
    n = w_kv.shape[1] // 2
    nj = n // tn
    vec = lambda c: pl.BlockSpec((1, c), lambda i, j: (0, 0))
    w_lo = pl.BlockSpec((d, tn), lambda i, j: (0, j))
    w_hi = pl.BlockSpec((d, tn), lambda i, j: (0, j + nj))
    out = pl.BlockSpec((tm, tn), lambda i, j: (i, j))
    act = jax.ShapeDtypeStruct((s, n), BF16)
    return pl.pallas_call(
        functools.partial(_qkv_kernel, q_scale=q_scale),
        grid=(s // tm, nj),
        in_specs=[pl.BlockSpec((tm, d), lambda i, j: (i, 0)), vec(d), vec(d), vec(HEAD_DIM), vec(HEAD_DIM),
                  w_lo, w_hi, w_lo, w_hi],
        out_specs=[pl.BlockSpec((tn, tm), lambda i, j: (j, i)), out, out, out],
        out_shape=[jax.ShapeDtypeStruct((n, s), BF16), act, act, act],
        scratch_shapes=[pltpu.VMEM((tm, d), BF16), pltpu.VMEM((tm, d), BF16)],
        compiler_params=_params("arbitrary", "arbitrary"),
        name="qkv_proj",
    )(x, kv_norm_g.reshape(1, d), b_norm_g.reshape(1, d), k_norm_g.reshape(1, HEAD_DIM),
      q_norm_g.reshape(1, HEAD_DIM), w_kv, w_kv, w_b, w_b)


def _visible(rows, cols):
    shift = CHUNK.bit_length() - 1
    q_chunk = lax.broadcasted_iota(jnp.int32, (rows, cols), 0) >> shift
    k_chunk = lax.broadcasted_iota(jnp.int32, (rows, cols), 1) >> shift
    return k_chunk <= q_chunk


def _attn_finish(lam_ref, sz_ref, subg_ref, o_ref, softmax_out, lam_init, chunk):
    lp = lam_ref[...]
    lam = (jnp.exp(jnp.sum(lp[0:1] * lp[1:2], axis=-1, keepdims=True))
           - jnp.exp(jnp.sum(lp[2:3] * lp[3:4], axis=-1, keepdims=True)) + lam_init)
    for r in range(0, o_ref.shape[0], chunk):
        rows = slice(r, r + chunk)
        o = softmax_out(0, rows) - lam * softmax_out(1, rows)
        ms = jnp.mean(o * o, axis=-1, keepdims=True)
        o = o * lax.rsqrt(ms + EPS) * subg_ref[...] * (1.0 - lam_init)
        o_ref[rows, :] = (o * sz_ref[rows, :].astype(F32)).astype(BF16)


def _map_cols(m):
    return slice(m * HEAD_DIM, (m + 1) * HEAD_DIM)


def _attn_online_kernel(lam_ref, q_ref, kt_ref, v_ref, sz_ref, subg_ref, o_ref, acc_ref, m_ref, l_ref,
                        *, t, lam_init):
    qi = pl.program_id(1)
    acc_ref[...] = jnp.zeros_like(acc_ref)
    l_ref[...] = jnp.zeros_like(l_ref)
    m_ref[...] = jnp.full_like(m_ref, NEG_BIG)

    def step(j, masked):
        cols = pl.ds(pl.multiple_of(j * t, t), t)
        kts = kt_ref[:, cols]
        vs = v_ref[cols, :]
        for m in range(2):
            sc = _dot(q_ref[:, _map_cols(m)], kts[_map_cols(m), :])
            if masked:
                sc = jnp.where(_visible(t, t), sc, NEG_BIG)
            m_prev = m_ref[m]
            m_next = jnp.maximum(m_prev, jnp.max(sc, axis=1, keepdims=True))
            p = jnp.exp2(sc - _lane_tile(m_next, t // LANES))
            alpha = jnp.exp2(m_prev - m_next)
            l_ref[m] = alpha * l_ref[m] + jnp.sum(p, axis=1, keepdims=True)
            m_ref[m] = m_next
            acc_ref[m] = acc_ref[m] * _lane_tile(alpha, V_DIM // LANES) + _dot(p.astype(BF16), vs)

    def body(j, carry):
        step(j, False)
        return carry

    lax.fori_loop(0, qi, body, 0)
    step(qi, True)

    _attn_finish(lam_ref, sz_ref, subg_ref, o_ref,
                 lambda m, rows: acc_ref[m, rows, :] * (1.0 / l_ref[m, rows, 0:1]), lam_init, t)


def _attn_bounded_kernel(lam_ref, q_ref, qn_ref, kt_ref, v_ref, sz_ref, subg_ref, o_ref, acc_ref, l_ref,
                         sa_ref, sb_ref, *, tq, tk, lam_init):
    qi = pl.program_id(1)
    last = pl.num_programs(1) - 1
    ratio = tq // tk
    assert ratio % 2 == 0, "the pair loop covers the qi * ratio fully visible kv blocks"
    acc_ref[...] = jnp.zeros_like(acc_ref)
    l_ref[...] = jnp.zeros_like(l_ref)

    def scores(j, s_ref, row0=0, queries=q_ref):
        kts = kt_ref[:, pl.ds(pl.multiple_of(j * tk, tk), tk)]
        for r in range(row0, tq, tk):
            for m in range(2):
                s_ref[m, r:r + tk, :] = _dot(queries[r:r + tk, _map_cols(m)], kts[_map_cols(m), :])

    def consume(j, s_ref, diag_row0=None):
        row0 = 0 if diag_row0 is None else diag_row0
        vs = v_ref[pl.ds(pl.multiple_of(j * tk, tk), tk), :]
        for r in range(row0, tq, tk):
            for m in range(2):
                sc = s_ref[m, r:r + tk, :]
                if r == diag_row0:
                    sc = jnp.where(_visible(tk, tk), sc, NEG_BIG)
                p = jnp.exp2(sc)
                part = p[:, 0:LANES]
                for c in range(1, tk // LANES):
                    part = part + p[:, c * LANES:(c + 1) * LANES]
                l_ref[m, r:r + tk, :] += part
                acc_ref[m, r:r + tk, :] += _dot(p.astype(BF16), vs)

    n_full = qi * ratio

    @pl.when(qi == 0)
    def _():
        scores(0, sa_ref)

    def pair(j):
        scores(j + 1, sb_ref)
        consume(j, sa_ref)
        scores(j + 2, sa_ref)
        consume(j + 1, sb_ref)

    def two_pairs(jj, carry):
        pair(4 * jj)
        pair(4 * jj + 2)
        return carry

    n_pairs = n_full // 2
    lax.fori_loop(0, n_pairs // 2, two_pairs, 0)

    @pl.when(n_pairs % 2 == 1)
    def _():
        pair(n_full - 2)

    bufs = (sa_ref, sb_ref)
    for d in range(ratio):
        if d + 1 < ratio:
            scores(n_full + d + 1, bufs[(d + 1) % 2], row0=(d + 1) * tk)
        consume(n_full + d, bufs[d % 2], diag_row0=d * tk)

    def finish():
        _attn_finish(lam_ref, sz_ref, subg_ref, o_ref,
                     lambda m, rows: acc_ref[m, rows, :] * (1.0 / jnp.sum(l_ref[m, rows, :], axis=1, keepdims=True)),
                     lam_init, tk)

    @pl.when(qi < last)
    def _():
        scores(0, sa_ref, queries=qn_ref)
        finish()

    @pl.when(qi == last)
    def _():
        finish()


def _attention(q, kt, v, sz, lam_p, subln_g, *, lam_init, bounded, tq=1024, tk=512):
    s, n = q.shape
    heads = n // V_DIM
    if bounded:
        body = functools.partial(_attn_bounded_kernel, tq=tq, tk=tk, lam_init=lam_init)
        scores = pltpu.VMEM((2, tq, tk), F32)
        scratch = [pltpu.VMEM((2, tq, V_DIM), F32), pltpu.VMEM((2, tq, LANES), F32), scores, scores]
    else:
        tq = tk
        body = functools.partial(_attn_online_kernel, t=tk, lam_init=lam_init)
        stats = pltpu.VMEM((2, tq, LANES), F32)
        scratch = [pltpu.VMEM((2, tq, V_DIM), F32), stats, stats]
    n_tiles = s // tq
    blk = pl.BlockSpec((tq, V_DIM), lambda h, i: (i, h))
    queries, query_specs = [q], [blk]
    if bounded:
        queries.append(q)
        query_specs.append(pl.BlockSpec((tq, V_DIM), lambda h, i: (jnp.minimum(i + 1, n_tiles - 1), h)))
    return pl.pallas_call(
        body,
        grid=(heads, n_tiles),
        in_specs=[pl.BlockSpec(lam_p.shape, lambda h, i: (0, 0)), *query_specs,
                  pl.BlockSpec((V_DIM, s), lambda h, i: (h, 0)), pl.BlockSpec((s, V_DIM), lambda h, i: (0, h)), blk,
                  pl.BlockSpec((1, V_DIM), lambda h, i: (0, 0))],
        out_specs=blk,
        out_shape=jax.ShapeDtypeStruct((s, n), BF16),
        scratch_shapes=scratch,
        compiler_params=_params("arbitrary", "arbitrary"),
        name="diff_attn_bounded" if bounded else "diff_attn_online",
    )(lam_p, *queries, kt, v, sz, subln_g.reshape(1, V_DIM))


def _out_proj_kernel(x_ref, a_ref, w_ref, o_ref):
    o_ref[...] = x_ref[...] + _dot(a_ref[...], w_ref[...])


def _out_proj(x, a, w, *, tm=512):
    s, d = x.shape
    n = a.shape[1]
    return pl.pallas_call(
        _out_proj_kernel,
        grid=(s // tm,),
        in_specs=[pl.BlockSpec((tm, d), lambda i: (i, 0)), pl.BlockSpec((tm, n), lambda i: (i, 0)),
                  pl.BlockSpec((n, d), lambda i: (0, 0))],
        out_specs=pl.BlockSpec((tm, d), lambda i: (i, 0)),
        out_shape=jax.ShapeDtypeStruct((s, d), F32),
        compiler_params=_params("arbitrary"),
        name="out_proj",
    )(x, a, w)


def kernel(x, a_norm_g, a_w_in, a_dw_w, a_dw_b, a_ln_g, a_ln_b, a_w_out, kv_norm_g, w_kv, k_norm_g,
           b_norm_g, b_w_in, b_q_norm_g, b_lambda, b_subln_g, b_w_out):
    bsz, s, d = x.shape
    n_a, n_b = a_norm_g.shape[0], b_norm_g.shape[0]
    assert bsz == 1 and n_a == 1 and n_b == 1, "one conformer layer followed by one attention layer"
    xs = x.reshape(s, d)

    y, sz, (a_w_out_b, w_kv_b, b_w_in_b, b_w_out_b) = _conf_in(
        xs, a_norm_g[0], a_w_in[0].astype(BF16), [a_w_out[0], w_kv, b_w_in[0], b_w_out[0]])
    xs = _conf_tail(xs, y, sz, a_dw_w[0], a_dw_b[0], a_ln_g[0], a_ln_b[0], a_w_out_b)

    layer_idx = n_a + 1
    lam_init = 0.8 - 0.6 * math.exp(-0.3 * (layer_idx - 1))
    q_scale = HEAD_DIM ** -0.5 * LOG2E
    kt, v, q, sz = _qkv(xs, kv_norm_g, b_norm_g[0], k_norm_g, b_q_norm_g[0], w_kv_b, b_w_in_b, q_scale=q_scale)
    score_bound = (HEAD_DIM ** 0.5 * LOG2E * BF16_MARGIN
                   * jnp.max(jnp.abs(b_q_norm_g[0])) * jnp.max(jnp.abs(k_norm_g)))
    attend = functools.partial(_attention, lam_init=lam_init)
    og = lax.cond(score_bound <= SCORE_BOUND,
                  functools.partial(attend, bounded=True), functools.partial(attend, bounded=False),
                  q, kt, v, sz, b_lambda[0], b_subln_g[0])
    out = _out_proj(xs, og, b_w_out_b)
    return out.reshape(bsz, s, d)
```

```python
import functools
import math

import jax
import jax.numpy as jnp
from jax import lax
from jax.experimental import pallas as pl
from jax.experimental.pallas import tpu as pltpu

EPS = 1e-6
CHUNK = 64
CONV_WIDTH = 31
HEAD_DIM = 128
V_DIM = 2 * HEAD_DIM
LANES = 128
SUBLANES = 8
BF16_SUBLANES = 16
HALO_ROWS = 32
VMEM_LIMIT = 60 * 1024 * 1024
LOG2E = 1.4426950408889634
NEG_BIG = -1e30
SCORE_BOUND = 64.0
BF16_MARGIN = 1.02

BF16 = jnp.bfloat16
F32 = jnp.float32


def _sigmoid(v):
    return 1.0 / (1.0 + jnp.exp(-v))


def _dot(a, b):
    return jnp.dot(a, b, preferred_element_type=F32)


def _lane_tile(a, n):
    return jnp.concatenate([a] * n, axis=1)


def _params(*sem):
    return pltpu.CompilerParams(dimension_semantics=sem, vmem_limit_bytes=VMEM_LIMIT)


def _conf_in_kernel(x_ref, g_ref, wa_ref, wb_ref, wz_ref, *rest, has_gain):
    n_later, n_gain = len(has_gain), sum(has_gain)
    later_f32, gains = rest[:n_later], list(rest[n_later:n_later + n_gain])
    y_ref, sz_ref = rest[n_later + n_gain:n_later + n_gain + 2]
    later_bf16, h_ref = rest[n_later + n_gain + 2:-1], rest[-1]

    @pl.when(pl.program_id(1) == 0)
    def _():
        x = x_ref[...]
        ms = jnp.mean(x * x, axis=-1, keepdims=True)
        h_ref[...] = (x * lax.rsqrt(ms + EPS) * g_ref[...]).astype(BF16)

    h = h_ref[...]
    gate = _sigmoid(_dot(h, wb_ref[...]))
    z = _dot(h, wz_ref[...])
    sz_ref[...] = (z * _sigmoid(z)).astype(BF16)
    y_ref[...] = (_dot(h, wa_ref[...]) * gate).astype(BF16)

    for src, dst, gained in zip(later_f32, later_bf16, has_gain):
        w = src[...]
        if gained:
            w = w * gains.pop(0)[...]
        dst[...] = w.astype(BF16)


def _conf_in(x, norm_g, w_in, later_weights, *, tm=512, tn=1024):
    s, d = x.shape
    e = w_in.shape[1] // 3
    nj = e // tn
    steps = (s // tm) * nj
    mats = [w for w, _ in later_weights]
    gains = [g.reshape(-1, 1) for _, g in later_weights if g is not None]
    slab = lambda w: pl.BlockSpec((w.shape[0] // steps, w.shape[1]), lambda i, j: (i * nj + j, 0))
    assert all(w.shape[0] % (steps * BF16_SUBLANES) == 0 for w in mats), "bf16 slabs are whole vregs"
    act = jax.ShapeDtypeStruct((s, e), BF16)
    outs = pl.pallas_call(
        functools.partial(_conf_in_kernel, has_gain=tuple(g is not None for _, g in later_weights)),
        grid=(s // tm, nj),
        in_specs=[
            pl.BlockSpec((tm, d), lambda i, j: (i, 0)),
            pl.BlockSpec((1, d), lambda i, j: (0, 0)),
            pl.BlockSpec((d, tn), lambda i, j: (0, j)),
            pl.BlockSpec((d, tn), lambda i, j: (0, j + nj)),
            pl.BlockSpec((d, tn), lambda i, j: (0, j + 2 * nj)),
            *[slab(w) for w in mats],
            *[slab(g) for g in gains],
        ],
        out_specs=[
            pl.BlockSpec((tm, tn), lambda i, j: (i, j)),
            pl.BlockSpec((tm, tn), lambda i, j: (i, j)),
            *[slab(w) for w in mats],
        ],
        out_shape=[act, act, *[jax.ShapeDtypeStruct(w.shape, BF16) for w in mats]],
        scratch_shapes=[pltpu.VMEM((tm, d), BF16)],
        compiler_params=_params("arbitrary", "arbitrary"),
        name="conf_in",
    )(x, norm_g.reshape(1, d), w_in, w_in, w_in, *mats, *gains)
    return outs[0], outs[1], outs[2:]


def _conf_tail_kernel(x_ref, ym_ref, yh_ref, sz_ref, dww_ref, dwb_ref, lng_ref, lnb_ref, wout_ref,
                      o_ref, ybuf_ref, cbuf_ref, *, tm, rows):
    e = ym_ref.shape[1]
    halo = yh_ref[...].astype(F32)
    ybuf_ref[0:HALO_ROWS, :] = jnp.where(pl.program_id(0) > 0, halo, 0.0)
    ybuf_ref[HALO_ROWS:, :] = ym_ref[...].astype(F32)
    first = HALO_ROWS - (CONV_WIDTH - 1)

    def lane_body(g, carry):
        lanes = pl.ds(pl.multiple_of(g * LANES, LANES), LANES)
        for rc in range(tm // rows):
            acc = jnp.broadcast_to(dwb_ref[:, lanes], (rows, LANES))
            for r in range(SUBLANES):
                span = rows + SUBLANES if r else rows
                z = None
                for k in range(CONV_WIDTH):
                    if (first + k) % SUBLANES != r:
                        continue
                    aligned = rc * rows + first + k - r
                    term = ybuf_ref[pl.ds(aligned, span), lanes] * dww_ref[pl.ds(k, 1), lanes]
                    z = term if z is None else z + term
                acc = acc + z[r:r + rows]
            cbuf_ref[pl.ds(rc * rows, rows), lanes] = acc
        return carry

    lax.fori_loop(0, e // LANES, lane_body, 0)

    c = cbuf_ref[...]
    mu = jnp.mean(c, axis=-1, keepdims=True)
    xc = c - mu
    var = jnp.mean(xc * xc, axis=-1, keepdims=True)
    yn = xc * lax.rsqrt(var + EPS) * lng_ref[...] + lnb_ref[...]
    act = yn * _sigmoid(yn) * sz_ref[...].astype(F32)
    o_ref[...] = x_ref[...] + _dot(act.astype(BF16), wout_ref[...])


def _conf_tail(x, y, sz, dw_w, dw_b, ln_g, ln_b, w_out, *, tm=256, rows=256):
    s, d = x.shape
    e = y.shape[1]
    hb = tm // HALO_ROWS
    row = lambda i: (i, 0)
    const = lambda i: (0, 0)
    return pl.pallas_call(
        functools.partial(_conf_tail_kernel, tm=tm, rows=rows),
        grid=(s // tm,),
        in_specs=[
            pl.BlockSpec((tm, d), row),
            pl.BlockSpec((tm, e), row),
            pl.BlockSpec((HALO_ROWS, e), lambda i: (jnp.maximum(i * hb - 1, 0), 0)),
            pl.BlockSpec((tm, e), row),
            pl.BlockSpec((CONV_WIDTH, e), const),
            pl.BlockSpec((1, e), const),
            pl.BlockSpec((1, e), const),
            pl.BlockSpec((1, e), const),
            pl.BlockSpec((e, d), const),
        ],
        out_specs=pl.BlockSpec((tm, d), row),
        out_shape=jax.ShapeDtypeStruct((s, d), F32),
        scratch_shapes=[pltpu.VMEM((tm + HALO_ROWS, e), F32), pltpu.VMEM((tm, e), F32)],
        compiler_params=_params("arbitrary"),
        name="conf_tail",
    )(x, y, y, sz, dw_w, dw_b.reshape(1, e), ln_g.reshape(1, e), ln_b.reshape(1, e), w_out)


def _group_rms(u, g_row, scale):
    outs = []
    for c in range(u.shape[1] // HEAD_DIM):
        blk = u[:, c * HEAD_DIM:(c + 1) * HEAD_DIM]
        ms = jnp.mean(blk * blk, axis=-1, keepdims=True)
        outs.append(blk * lax.rsqrt(ms + EPS) * (g_row * scale))
    return jnp.concatenate(outs, axis=1)


def _qkv_kernel(x_ref, kg_ref, qg_ref, wk_ref, wv_ref, wq_ref, wz_ref,
                kt_ref, v_ref, q_ref, sz_ref, h_ref, *, q_scale):
    @pl.when(pl.program_id(1) == 0)
    def _():
        x = x_ref[...]
        h_ref[...] = (x * lax.rsqrt(jnp.mean(x * x, axis=-1, keepdims=True) + EPS)).astype(BF16)

    h = h_ref[...]
    kt_ref[...] = _group_rms(_dot(h, wk_ref[...]), kg_ref[...], 1.0).T.astype(BF16)
    q_ref[...] = _group_rms(_dot(h, wq_ref[...]), qg_ref[...], q_scale).astype(BF16)
    z = _dot(h, wz_ref[...])
    sz_ref[...] = (z * _sigmoid(z)).astype(BF16)
    v_ref[...] = _dot(h, wv_ref[...]).astype(BF16)


def _qkv(x, k_norm_g, q_norm_g, w_kv, w_b, *, q_scale, tm=512, tn=512):
    s, d = x.shape
    n = w_kv.shape[1] // 2
    nj = n // tn
    vec = lambda c: pl.BlockSpec((1, c), lambda i, j: (0, 0))
    w_lo = pl.BlockSpec((d, tn), lambda i, j: (0, j))
    w_hi = pl.BlockSpec((d, tn), lambda i, j: (0, j + nj))
    out = pl.BlockSpec((tm, tn), lambda i, j: (i, j))
    act = jax.ShapeDtypeStruct((s, n), BF16)
    return pl.pallas_call(
        functools.partial(_qkv_kernel, q_scale=q_scale),
        grid=(s // tm, nj),
        in_specs=[pl.BlockSpec((tm, d), lambda i, j: (i, 0)), vec(HEAD_DIM), vec(HEAD_DIM),
                  w_lo, w_hi, w_lo, w_hi],
        out_specs=[pl.BlockSpec((tn, tm), lambda i, j: (j, i)), out, out, out],
        out_shape=[jax.ShapeDtypeStruct((n, s), BF16), act, act, act],
        scratch_shapes=[pltpu.VMEM((tm, d), BF16)],
        compiler_params=_params("arbitrary", "arbitrary"),
        name="qkv_proj",
    )(x, k_norm_g.reshape(1, HEAD_DIM), q_norm_g.reshape(1, HEAD_DIM), w_kv, w_kv, w_b, w_b)


def _visible(rows, cols):
    shift = CHUNK.bit_length() - 1
    q_chunk = lax.broadcasted_iota(jnp.int32, (rows, cols), 0) >> shift
    k_chunk = lax.broadcasted_iota(jnp.int32, (rows, cols), 1) >> shift
    return k_chunk <= q_chunk


def _attn_finish(lam_ref, sz_ref, subg_ref, o_ref, softmax_out, lam_init, chunk):
    lp = lam_ref[...]
    lam = (jnp.exp(jnp.sum(lp[0:1] * lp[1:2], axis=-1, keepdims=True))
           - jnp.exp(jnp.sum(lp[2:3] * lp[3:4], axis=-1, keepdims=True)) + lam_init)
    for r in range(0, o_ref.shape[0], chunk):
        rows = slice(r, r + chunk)
        o = softmax_out(0, rows) - lam * softmax_out(1, rows)
        ms = jnp.mean(o * o, axis=-1, keepdims=True)
        o = o * lax.rsqrt(ms + EPS) * subg_ref[...] * (1.0 - lam_init)
        o_ref[rows, :] = (o * sz_ref[rows, :].astype(F32)).astype(BF16)


def _map_cols(m):
    return slice(m * HEAD_DIM, (m + 1) * HEAD_DIM)


def _attn_online_kernel(lam_ref, q_ref, kt_ref, v_ref, sz_ref, subg_ref, o_ref, acc_ref, m_ref, l_ref,
                        *, t, lam_init):
    qi = pl.program_id(1)
    acc_ref[...] = jnp.zeros_like(acc_ref)
    l_ref[...] = jnp.zeros_like(l_ref)
    m_ref[...] = jnp.full_like(m_ref, NEG_BIG)

    def step(j, masked):
        cols = pl.ds(pl.multiple_of(j * t, t), t)
        kts = kt_ref[:, cols]
        vs = v_ref[cols, :]
        for m in range(2):
            sc = _dot(q_ref[:, _map_cols(m)], kts[_map_cols(m), :])
            if masked:
                sc = jnp.where(_visible(t, t), sc, NEG_BIG)
            m_prev = m_ref[m]
            m_next = jnp.maximum(m_prev, jnp.max(sc, axis=1, keepdims=True))
            p = jnp.exp2(sc - _lane_tile(m_next, t // LANES))
            alpha = jnp.exp2(m_prev - m_next)
            l_ref[m] = alpha * l_ref[m] + jnp.sum(p, axis=1, keepdims=True)
            m_ref[m] = m_next
            acc_ref[m] = acc_ref[m] * _lane_tile(alpha, V_DIM // LANES) + _dot(p.astype(BF16), vs)

    def body(j, carry):
        step(j, False)
        return carry

    lax.fori_loop(0, qi, body, 0)
    step(qi, True)

    _attn_finish(lam_ref, sz_ref, subg_ref, o_ref,
                 lambda m, rows: acc_ref[m, rows, :] * (1.0 / l_ref[m, rows, 0:1]), lam_init, t)


def _attn_bounded_kernel(lam_ref, q_ref, qn_ref, kt_ref, v_ref, sz_ref, subg_ref, o_ref, acc_ref, l_ref,
                         sa_ref, sb_ref, *, tq, tk, lam_init):
    qi = pl.program_id(1)
    last = pl.num_programs(1) - 1
    ratio = tq // tk
    assert ratio % 2 == 0, "the pair loop covers the qi * ratio fully visible kv blocks"
    acc_ref[...] = jnp.zeros_like(acc_ref)
    l_ref[...] = jnp.zeros_like(l_ref)

    def scores(j, s_ref, row0=0, queries=q_ref):
        kts = kt_ref[:, pl.ds(pl.multiple_of(j * tk, tk), tk)]
        for r in range(row0, tq, tk):
            for m in range(2):
                s_ref[m, r:r + tk, :] = _dot(queries[r:r + tk, _map_cols(m)], kts[_map_cols(m), :])

    def consume(j, s_ref, diag_row0=None):
        row0 = 0 if diag_row0 is None else diag_row0
        vs = v_ref[pl.ds(pl.multiple_of(j * tk, tk), tk), :]
        for r in range(row0, tq, tk):
            for m in range(2):
                sc = s_ref[m, r:r + tk, :]
                if r == diag_row0:
                    sc = jnp.where(_visible(tk, tk), sc, NEG_BIG)
                p = jnp.exp2(sc)
                part = p[:, 0:LANES]
                for c in range(1, tk // LANES):
                    part = part + p[:, c * LANES:(c + 1) * LANES]
                l_ref[m, r:r + tk, :] += part
                acc_ref[m, r:r + tk, :] += _dot(p.astype(BF16), vs)

    n_full = qi * ratio

    @pl.when(qi == 0)
    def _():
        scores(0, sa_ref)

    def pair(j):
        scores(j + 1, sb_ref)
        consume(j, sa_ref)
        scores(j + 2, sa_ref)
        consume(j + 1, sb_ref)

    def two_pairs(jj, carry):
        pair(4 * jj)
        pair(4 * jj + 2)
        return carry

    n_pairs = n_full // 2
    lax.fori_loop(0, n_pairs // 2, two_pairs, 0)

    @pl.when(n_pairs % 2 == 1)
    def _():
        pair(n_full - 2)

    bufs = (sa_ref, sb_ref)
    for d in range(ratio):
        if d + 1 < ratio:
            scores(n_full + d + 1, bufs[(d + 1) % 2], row0=(d + 1) * tk)
        consume(n_full + d, bufs[d % 2], diag_row0=d * tk)

    def finish():
        _attn_finish(lam_ref, sz_ref, subg_ref, o_ref,
                     lambda m, rows: acc_ref[m, rows, :] * (1.0 / jnp.sum(l_ref[m, rows, :], axis=1, keepdims=True)),
                     lam_init, tk)

    @pl.when(qi < last)
    def _():
        scores(0, sa_ref, queries=qn_ref)
        finish()

    @pl.when(qi == last)
    def _():
        finish()


def _attention(q, kt, v, sz, lam_p, subln_g, *, lam_init, bounded, tq=1024, tk=512):
    s, n = q.shape
    heads = n // V_DIM
    if bounded:
        body = functools.partial(_attn_bounded_kernel, tq=tq, tk=tk, lam_init=lam_init)
        scores = pltpu.VMEM((2, tq, tk), F32)
        scratch = [pltpu.VMEM((2, tq, V_DIM), F32), pltpu.VMEM((2, tq, LANES), F32), scores, scores]
    else:
        tq = tk
        body = functools.partial(_attn_online_kernel, t=tk, lam_init=lam_init)
        stats = pltpu.VMEM((2, tq, LANES), F32)
        scratch = [pltpu.VMEM((2, tq, V_DIM), F32), stats, stats]
    n_tiles = s // tq
    blk = pl.BlockSpec((tq, V_DIM), lambda h, i: (i, h))
    queries, query_specs = [q], [blk]
    if bounded:
        queries.append(q)
        query_specs.append(pl.BlockSpec((tq, V_DIM), lambda h, i: (jnp.minimum(i + 1, n_tiles - 1), h)))
    return pl.pallas_call(
        body,
        grid=(heads, n_tiles),
        in_specs=[pl.BlockSpec(lam_p.shape, lambda h, i: (0, 0)), *query_specs,
                  pl.BlockSpec((V_DIM, s), lambda h, i: (h, 0)), pl.BlockSpec((s, V_DIM), lambda h, i: (0, h)), blk,
                  pl.BlockSpec((1, V_DIM), lambda h, i: (0, 0))],
        out_specs=blk,
        out_shape=jax.ShapeDtypeStruct((s, n), BF16),
        scratch_shapes=scratch,
        compiler_params=_params("arbitrary", "arbitrary"),
        name="diff_attn_bounded" if bounded else "diff_attn_online",
    )(lam_p, *queries, kt, v, sz, subln_g.reshape(1, V_DIM))


def _out_proj_kernel(x_ref, a_ref, w_ref, o_ref):
    o_ref[...] = x_ref[...] + _dot(a_ref[...], w_ref[...])


def _out_proj(x, a, w, *, tm=512):
    s, d = x.shape
    n = a.shape[1]
    return pl.pallas_call(
        _out_proj_kernel,
        grid=(s // tm,),
        in_specs=[pl.BlockSpec((tm, d), lambda i: (i, 0)), pl.BlockSpec((tm, n), lambda i: (i, 0)),
                  pl.BlockSpec((n, d), lambda i: (0, 0))],
        out_specs=pl.BlockSpec((tm, d), lambda i: (i, 0)),
        out_shape=jax.ShapeDtypeStruct((s, d), F32),
        compiler_params=_params("arbitrary"),
        name="out_proj",
    )(x, a, w)


def kernel(x, a_norm_g, a_w_in, a_dw_w, a_dw_b, a_ln_g, a_ln_b, a_w_out, kv_norm_g, w_kv, k_norm_g,
           b_norm_g, b_w_in, b_q_norm_g, b_lambda, b_subln_g, b_w_out):
    bsz, s, d = x.shape
    n_a, n_b = a_norm_g.shape[0], b_norm_g.shape[0]
    assert bsz == 1 and n_a == 1 and n_b == 1, "one conformer layer followed by one attention layer"
    xs = x.reshape(s, d)

    y, sz, (a_w_out_b, w_kv_b, b_w_in_b, b_w_out_b) = _conf_in(
        xs, a_norm_g[0], a_w_in[0].astype(BF16),
        [(a_w_out[0], None), (w_kv, kv_norm_g), (b_w_in[0], b_norm_g[0]), (b_w_out[0], None)])
    xs = _conf_tail(xs, y, sz, a_dw_w[0], a_dw_b[0], a_ln_g[0], a_ln_b[0], a_w_out_b)

    layer_idx = n_a + 1
    lam_init = 0.8 - 0.6 * math.exp(-0.3 * (layer_idx - 1))
    q_scale = HEAD_DIM ** -0.5 * LOG2E
    kt, v, q, sz = _qkv(xs, k_norm_g, b_q_norm_g[0], w_kv_b, b_w_in_b, q_scale=q_scale)
    score_bound = (HEAD_DIM ** 0.5 * LOG2E * BF16_MARGIN
                   * jnp.max(jnp.abs(b_q_norm_g[0])) * jnp.max(jnp.abs(k_norm_g)))
    attend = functools.partial(_attention, lam_init=lam_init)
    og = lax.cond(score_bound <= SCORE_BOUND,
                  functools.partial(attend, bounded=True), functools.partial(attend, bounded=False),
                  q, kt, v, sz, b_lambda[0], b_subln_g[0])
    out = _out_proj(xs, og, b_w_out_b)
    return out.reshape(bsz, s, d)
```

```python
import functools
import math

import jax
import jax.numpy as jnp
from jax import lax
from jax.experimental import pallas as pl
from jax.experimental.pallas import tpu as pltpu

EPS = 1e-6
CHUNK = 64
CONV_WIDTH = 31
HEAD_DIM = 128
V_DIM = 2 * HEAD_DIM
LANES = 128
SUBLANES = 8
BF16_SUBLANES = 16
HALO_ROWS = 32
VMEM_LIMIT = 60 * 1024 * 1024
LOG2E = 1.4426950408889634
NEG_BIG = -1e30
SCORE_BOUND = 64.0
BF16_MARGIN = 1.02

BF16 = jnp.bfloat16
F32 = jnp.float32


def _sigmoid(v):
    return 1.0 / (1.0 + jnp.exp(-v))


def _dot(a, b):
    return jnp.dot(a, b, preferred_element_type=F32)


def _lane_tile(a, n):
    return jnp.concatenate([a] * n, axis=1)


def _params(*sem):
    return pltpu.CompilerParams(dimension_semantics=sem, vmem_limit_bytes=VMEM_LIMIT)


def _conf_in_kernel(x_ref, g_ref, wa_ref, wb_ref, wz_ref, *rest, has_gain):
    n_later, n_gain = len(has_gain), sum(has_gain)
    later_f32, gains = rest[:n_later], list(rest[n_later:n_later + n_gain])
    y_ref, sz_ref = rest[n_later + n_gain:n_later + n_gain + 2]
    later_bf16, h_ref = rest[n_later + n_gain + 2:-1], rest[-1]

    @pl.when(pl.program_id(1) == 0)
    def _():
        x = x_ref[...]
        ms = jnp.mean(x * x, axis=-1, keepdims=True)
        h_ref[...] = (x * lax.rsqrt(ms + EPS) * g_ref[...]).astype(BF16)

    h = h_ref[...]
    gate = _sigmoid(_dot(h, wb_ref[...]))
    z = _dot(h, wz_ref[...])
    sz_ref[...] = (z * _sigmoid(z)).astype(BF16)
    y_ref[...] = (_dot(h, wa_ref[...]) * gate).astype(BF16)

    for src, dst, gained in zip(later_f32, later_bf16, has_gain):
        w = src[...]
        if gained:
            w = w * gains.pop(0)[...]
        dst[...] = w.astype(BF16)


def _conf_in(x, norm_g, w_in, later_weights, *, tm=512, tn=1024):
    s, d = x.shape
    e = w_in.shape[1] // 3
    nj = e // tn
    steps = (s // tm) * nj
    mats = [w for w, _ in later_weights]
    gains = [g.reshape(-1, 1) for _, g in later_weights if g is not None]
    slab = lambda w: pl.BlockSpec((w.shape[0] // steps, w.shape[1]), lambda i, j: (i * nj + j, 0))
    assert all(w.shape[0] % (steps * BF16_SUBLANES) == 0 for w in mats), "bf16 slabs are whole vregs"
    act = jax.ShapeDtypeStruct((s, e), BF16)
    outs = pl.pallas_call(
        functools.partial(_conf_in_kernel, has_gain=tuple(g is not None for _, g in later_weights)),
        grid=(s // tm, nj),
        in_specs=[
            pl.BlockSpec((tm, d), lambda i, j: (i, 0)),
            pl.BlockSpec((1, d), lambda i, j: (0, 0)),
            pl.BlockSpec((d, tn), lambda i, j: (0, j)),
            pl.BlockSpec((d, tn), lambda i, j: (0, j + nj)),
            pl.BlockSpec((d, tn), lambda i, j: (0, j + 2 * nj)),
            *[slab(w) for w in mats],
            *[slab(g) for g in gains],
        ],
        out_specs=[
            pl.BlockSpec((tm, tn), lambda i, j: (i, j)),
            pl.BlockSpec((tm, tn), lambda i, j: (i, j)),
            *[slab(w) for w in mats],
        ],
        out_shape=[act, act, *[jax.ShapeDtypeStruct(w.shape, BF16) for w in mats]],
        scratch_shapes=[pltpu.VMEM((tm, d), BF16)],
        compiler_params=_params("arbitrary", "arbitrary"),
        name="conf_in",
    )(x, norm_g.reshape(1, d), w_in, w_in, w_in, *mats, *gains)
    return outs[0], outs[1], outs[2:]


def _conf_tail_kernel(x_ref, ym_ref, yh_ref, sz_ref, dww_ref, dwb_ref, lng_ref, lnb_ref, wout_ref,
                      o_ref, ybuf_ref, cbuf_ref, *, tm, rows):
    e = ym_ref.shape[1]
    halo = yh_ref[...].astype(F32)
    ybuf_ref[0:HALO_ROWS, :] = jnp.where(pl.program_id(0) > 0, halo, 0.0)
    ybuf_ref[HALO_ROWS:, :] = ym_ref[...].astype(F32)
    first = HALO_ROWS - (CONV_WIDTH - 1)

    def lane_body(g, carry):
        lanes = pl.ds(pl.multiple_of(g * LANES, LANES), LANES)
        for rc in range(tm // rows):
            acc = jnp.broadcast_to(dwb_ref[:, lanes], (rows, LANES))
            for r in range(SUBLANES):
                span = rows + SUBLANES if r else rows
                z = None
                for k in range(CONV_WIDTH):
                    if (first + k) % SUBLANES != r:
                        continue
                    aligned = rc * rows + first + k - r
                    term = ybuf_ref[pl.ds(aligned, span), lanes] * dww_ref[pl.ds(k, 1), lanes]
                    z = term if z is None else z + term
                acc = acc + z[r:r + rows]
            cbuf_ref[pl.ds(rc * rows, rows), lanes] = acc
        return carry

    lax.fori_loop(0, e // LANES, lane_body, 0)

    c = cbuf_ref[...]
    mu = jnp.mean(c, axis=-1, keepdims=True)
    xc = c - mu
    var = jnp.mean(xc * xc, axis=-1, keepdims=True)
    yn = xc * lax.rsqrt(var + EPS) * lng_ref[...] + lnb_ref[...]
    act = yn * _sigmoid(yn) * sz_ref[...].astype(F32)
    o_ref[...] = x_ref[...] + _dot(act.astype(BF16), wout_ref[...])


def _conf_tail(x, y, sz, dw_w, dw_b, ln_g, ln_b, w_out, *, tm=512, rows=256):
    s, d = x.shape
    e = y.shape[1]
    hb = tm // HALO_ROWS
    row = lambda i: (i, 0)
    const = lambda i: (0, 0)
    return pl.pallas_call(
        functools.partial(_conf_tail_kernel, tm=tm, rows=rows),
        grid=(s // tm,),
        in_specs=[
            pl.BlockSpec((tm, d), row),
            pl.BlockSpec((tm, e), row),
            pl.BlockSpec((HALO_ROWS, e), lambda i: (jnp.maximum(i * hb - 1, 0), 0)),
            pl.BlockSpec((tm, e), row),
            pl.BlockSpec((CONV_WIDTH, e), const),
            pl.BlockSpec((1, e), const),
            pl.BlockSpec((1, e), const),
            pl.BlockSpec((1, e), const),
            pl.BlockSpec((e, d), const, pipeline_mode=pl.Buffered(1)),
        ],
        out_specs=pl.BlockSpec((tm, d), row),
        out_shape=jax.ShapeDtypeStruct((s, d), F32),
        scratch_shapes=[pltpu.VMEM((tm + HALO_ROWS, e), F32), pltpu.VMEM((tm, e), F32)],
        compiler_params=_params("arbitrary"),
        name="conf_tail",
    )(x, y, y, sz, dw_w, dw_b.reshape(1, e), ln_g.reshape(1, e), ln_b.reshape(1, e), w_out)


def _group_rms(u, g_row, scale):
    outs = []
    for c in range(u.shape[1] // HEAD_DIM):
        blk = u[:, c * HEAD_DIM:(c + 1) * HEAD_DIM]
        ms = jnp.mean(blk * blk, axis=-1, keepdims=True)
        outs.append(blk * lax.rsqrt(ms + EPS) * (g_row * scale))
    return jnp.concatenate(outs, axis=1)


def _qkv_kernel(x_ref, kg_ref, qg_ref, wk_ref, wv_ref, wq_ref, wz_ref,
                kt_ref, v_ref, q_ref, sz_ref, h_ref, *, q_scale):
    @pl.when(pl.program_id(1) == 0)
    def _():
        x = x_ref[...]
        h_ref[...] = (x * lax.rsqrt(jnp.mean(x * x, axis=-1, keepdims=True) + EPS)).astype(BF16)

    h = h_ref[...]
    kt_ref[...] = _group_rms(_dot(h, wk_ref[...]), kg_ref[...], 1.0).T.astype(BF16)
    q_ref[...] = _group_rms(_dot(h, wq_ref[...]), qg_ref[...], q_scale).astype(BF16)
    z = _dot(h, wz_ref[...])
    sz_ref[...] = (z * _sigmoid(z)).astype(BF16)
    v_ref[...] = _dot(h, wv_ref[...]).astype(BF16)


def _qkv(x, k_norm_g, q_norm_g, w_kv, w_b, *, q_scale, tm=512, tn=512):
    s, d = x.shape
    n = w_kv.shape[1] // 2
    nj = n // tn
    vec = lambda c: pl.BlockSpec((1, c), lambda i, j: (0, 0))
    w_lo = pl.BlockSpec((d, tn), lambda i, j: (0, j))
    w_hi = pl.BlockSpec((d, tn), lambda i, j: (0, j + nj))
    out = pl.BlockSpec((tm, tn), lambda i, j: (i, j))
    act = jax.ShapeDtypeStruct((s, n), BF16)
    return pl.pallas_call(
        functools.partial(_qkv_kernel, q_scale=q_scale),
        grid=(s // tm, nj),
        in_specs=[pl.BlockSpec((tm, d), lambda i, j: (i, 0)), vec(HEAD_DIM), vec(HEAD_DIM),
                  w_lo, w_hi, w_lo, w_hi],
        out_specs=[pl.BlockSpec((tn, tm), lambda i, j: (j, i)), out, out, out],
        out_shape=[jax.ShapeDtypeStruct((n, s), BF16), act, act, act],
        scratch_shapes=[pltpu.VMEM((tm, d), BF16)],
        compiler_params=_params("arbitrary", "arbitrary"),
        name="qkv_proj",
    )(x, k_norm_g.reshape(1, HEAD_DIM), q_norm_g.reshape(1, HEAD_DIM), w_kv, w_kv, w_b, w_b)


def _visible(rows, cols):
    shift = CHUNK.bit_length() - 1
    q_chunk = lax.broadcasted_iota(jnp.int32, (rows, cols), 0) >> shift
    k_chunk = lax.broadcasted_iota(jnp.int32, (rows, cols), 1) >> shift
    return k_chunk <= q_chunk


def _attn_finish(lam_ref, sz_ref, subg_ref, o_ref, softmax_out, lam_init, chunk):
    lp = lam_ref[...]
    lam = (jnp.exp(jnp.sum(lp[0:1] * lp[1:2], axis=-1, keepdims=True))
           - jnp.exp(jnp.sum(lp[2:3] * lp[3:4], axis=-1, keepdims=True)) + lam_init)
    for r in range(0, o_ref.shape[0], chunk):
        rows = slice(r, r + chunk)
        o = softmax_out(0, rows) - lam * softmax_out(1, rows)
        ms = jnp.mean(o * o, axis=-1, keepdims=True)
        o = o * lax.rsqrt(ms + EPS) * subg_ref[...] * (1.0 - lam_init)
        o_ref[rows, :] = (o * sz_ref[rows, :].astype(F32)).astype(BF16)


def _map_cols(m):
    return slice(m * HEAD_DIM, (m + 1) * HEAD_DIM)


def _attn_online_kernel(lam_ref, q_ref, kt_ref, v_ref, sz_ref, subg_ref, o_ref, acc_ref, m_ref, l_ref,
                        *, t, lam_init):
    qi = pl.program_id(1)
    acc_ref[...] = jnp.zeros_like(acc_ref)
    l_ref[...] = jnp.zeros_like(l_ref)
    m_ref[...] = jnp.full_like(m_ref, NEG_BIG)

    def step(j, masked):
        cols = pl.ds(pl.multiple_of(j * t, t), t)
        kts = kt_ref[:, cols]
        vs = v_ref[cols, :]
        for m in range(2):
            sc = _dot(q_ref[:, _map_cols(m)], kts[_map_cols(m), :])
            if masked:
                sc = jnp.where(_visible(t, t), sc, NEG_BIG)
            m_prev = m_ref[m]
            m_next = jnp.maximum(m_prev, jnp.max(sc, axis=1, keepdims=True))
            p = jnp.exp2(sc - _lane_tile(m_next, t // LANES))
            alpha = jnp.exp2(m_prev - m_next)
            l_ref[m] = alpha * l_ref[m] + jnp.sum(p, axis=1, keepdims=True)
            m_ref[m] = m_next
            acc_ref[m] = acc_ref[m] * _lane_tile(alpha, V_DIM // LANES) + _dot(p.astype(BF16), vs)

    def body(j, carry):
        step(j, False)
        return carry

    lax.fori_loop(0, qi, body, 0)
    step(qi, True)

    _attn_finish(lam_ref, sz_ref, subg_ref, o_ref,
                 lambda m, rows: acc_ref[m, rows, :] * (1.0 / l_ref[m, rows, 0:1]), lam_init, t)


def _attn_bounded_kernel(lam_ref, q_ref, qn_ref, kt_ref, v_ref, sz_ref, subg_ref, o_ref, acc_ref, l_ref,
                         sa_ref, sb_ref, *, tq, tk, lam_init):
    qi = pl.program_id(1)
    last = pl.num_programs(1) - 1
    ratio = tq // tk
    assert ratio % 2 == 0, "the pair loop covers the qi * ratio fully visible kv blocks"
    acc_ref[...] = jnp.zeros_like(acc_ref)
    l_ref[...] = jnp.zeros_like(l_ref)

    def scores(j, s_ref, row0=0, queries=q_ref):
        kts = kt_ref[:, pl.ds(pl.multiple_of(j * tk, tk), tk)]
        for r in range(row0, tq, tk):
            for m in range(2):
                s_ref[m, r:r + tk, :] = _dot(queries[r:r + tk, _map_cols(m)], kts[_map_cols(m), :])

    def consume(j, s_ref, diag_row0=None):
        row0 = 0 if diag_row0 is None else diag_row0
        vs = v_ref[pl.ds(pl.multiple_of(j * tk, tk), tk), :]
        for r in range(row0, tq, tk):
            for m in range(2):
                sc = s_ref[m, r:r + tk, :]
                if r == diag_row0:
                    sc = jnp.where(_visible(tk, tk), sc, NEG_BIG)
                p = jnp.exp2(sc)
                part = p[:, 0:LANES]
                for c in range(1, tk // LANES):
                    part = part + p[:, c * LANES:(c + 1) * LANES]
                l_ref[m, r:r + tk, :] += part
                acc_ref[m, r:r + tk, :] += _dot(p.astype(BF16), vs)

    n_full = qi * ratio

    @pl.when(qi == 0)
    def _():
        scores(0, sa_ref)

    def pair(j):
        scores(j + 1, sb_ref)
        consume(j, sa_ref)
        scores(j + 2, sa_ref)
        consume(j + 1, sb_ref)

    def two_pairs(jj, carry):
        pair(4 * jj)
        pair(4 * jj + 2)
        return carry

    n_pairs = n_full // 2
    lax.fori_loop(0, n_pairs // 2, two_pairs, 0)

    @pl.when(n_pairs % 2 == 1)
    def _():
        pair(n_full - 2)

    bufs = (sa_ref, sb_ref)
    for d in range(ratio):
        if d + 1 < ratio:
            scores(n_full + d + 1, bufs[(d + 1) % 2], row0=(d + 1) * tk)
        consume(n_full + d, bufs[d % 2], diag_row0=d * tk)

    def finish():
        _attn_finish(lam_ref, sz_ref, subg_ref, o_ref,
                     lambda m, rows: acc_ref[m, rows, :] * (1.0 / jnp.sum(l_ref[m, rows, :], axis=1, keepdims=True)),
                     lam_init, tk)

    @pl.when(qi < last)
    def _():
        scores(0, sa_ref, queries=qn_ref)
        finish()

    @pl.when(qi == last)
    def _():
        finish()


def _attention(q, kt, v, sz, lam_p, subln_g, *, lam_init, bounded, tq=1024, tk=512):
    s, n = q.shape
    heads = n // V_DIM
    if bounded:
        body = functools.partial(_attn_bounded_kernel, tq=tq, tk=tk, lam_init=lam_init)
        scores = pltpu.VMEM((2, tq, tk), F32)
        scratch = [pltpu.VMEM((2, tq, V_DIM), F32), pltpu.VMEM((2, tq, LANES), F32), scores, scores]
    else:
        tq = tk
        body = functools.partial(_attn_online_kernel, t=tk, lam_init=lam_init)
        stats = pltpu.VMEM((2, tq, LANES), F32)
        scratch = [pltpu.VMEM((2, tq, V_DIM), F32), stats, stats]
    n_tiles = s // tq
    blk = pl.BlockSpec((tq, V_DIM), lambda h, i: (i, h))
    queries, query_specs = [q], [blk]
    if bounded:
        queries.append(q)
        query_specs.append(pl.BlockSpec((tq, V_DIM), lambda h, i: (jnp.minimum(i + 1, n_tiles - 1), h)))
    return pl.pallas_call(
        body,
        grid=(heads, n_tiles),
        in_specs=[pl.BlockSpec(lam_p.shape, lambda h, i: (0, 0)), *query_specs,
                  pl.BlockSpec((V_DIM, s), lambda h, i: (h, 0)), pl.BlockSpec((s, V_DIM), lambda h, i: (0, h)), blk,
                  pl.BlockSpec((1, V_DIM), lambda h, i: (0, 0))],
        out_specs=blk,
        out_shape=jax.ShapeDtypeStruct((s, n), BF16),
        scratch_shapes=scratch,
        compiler_params=_params("arbitrary", "arbitrary"),
        name="diff_attn_bounded" if bounded else "diff_attn_online",
    )(lam_p, *queries, kt, v, sz, subln_g.reshape(1, V_DIM))


def _out_proj_kernel(x_ref, a_ref, w_ref, o_ref):
    o_ref[...] = x_ref[...] + _dot(a_ref[...], w_ref[...])


def _out_proj(x, a, w, *, tm=512):
    s, d = x.shape
    n = a.shape[1]
    return pl.pallas_call(
        _out_proj_kernel,
        grid=(s // tm,),
        in_specs=[pl.BlockSpec((tm, d), lambda i: (i, 0)), pl.BlockSpec((tm, n), lambda i: (i, 0)),
                  pl.BlockSpec((n, d), lambda i: (0, 0))],
        out_specs=pl.BlockSpec((tm, d), lambda i: (i, 0)),
        out_shape=jax.ShapeDtypeStruct((s, d), F32),
        compiler_params=_params("arbitrary"),
        name="out_proj",
    )(x, a, w)


def kernel(x, a_norm_g, a_w_in, a_dw_w, a_dw_b, a_ln_g, a_ln_b, a_w_out, kv_norm_g, w_kv, k_norm_g,
           b_norm_g, b_w_in, b_q_norm_g, b_lambda, b_subln_g, b_w_out):
    bsz, s, d = x.shape
    n_a, n_b = a_norm_g.shape[0], b_norm_g.shape[0]
    assert bsz == 1 and n_a == 1 and n_b == 1, "one conformer layer followed by one attention layer"
    xs = x.reshape(s, d)

    y, sz, (a_w_out_b, w_kv_b, b_w_in_b, b_w_out_b) = _conf_in(
        xs, a_norm_g[0], a_w_in[0].astype(BF16),
        [(a_w_out[0], None), (w_kv, kv_norm_g), (b_w_in[0], b_norm_g[0]), (b_w_out[0], None)])
    xs = _conf_tail(xs, y, sz, a_dw_w[0], a_dw_b[0], a_ln_g[0], a_ln_b[0], a_w_out_b)

    layer_idx = n_a + 1
    lam_init = 0.8 - 0.6 * math.exp(-0.3 * (layer_idx - 1))
    q_scale = HEAD_DIM ** -0.5 * LOG2E
    kt, v, q, sz = _qkv(xs, k_norm_g, b_q_norm_g[0], w_kv_b, b_w_in_b, q_scale=q_scale)
    score_bound = (HEAD_DIM ** 0.5 * LOG2E * BF16_MARGIN
                   * jnp.max(jnp.abs(b_q_norm_g[0])) * jnp.max(jnp.abs(k_norm_g)))
    attend = functools.partial(_attention, lam_init=lam_init)
    og = lax.cond(score_bound <= SCORE_BOUND,
                  functools.partial(attend, bounded=True), functools.partial(attend, bounded=False),
                  q, kt, v, sz, b_lambda[0], b_subln_g[0])
    out = _out_proj(xs, og, b_w_out_b)
    return out.reshape(bsz, s, d)
```
